```python
import jax, jax.numpy as jnp
from jax import lax
import numpy as np

D_MODEL = 2048
BATCH = 4
SEQ = 4096
DEPTH = 1

CHUNK = 64
A_HEADS = 16
A_HEAD_DIM = 64
A_WIDTH = A_HEADS * A_HEAD_DIM
A_PAST_CHUNKS = 8
A_BAND = (A_PAST_CHUNKS + 1) * CHUNK
REL_CLIP = 256
REL_SIZE = REL_CLIP + CHUNK
B_HEADS = 4
B_KEY_DIM = D_MODEL // 4
B_VAL_DIM = D_MODEL // 2
B_HK = B_KEY_DIM // B_HEADS
B_HV = B_VAL_DIM // B_HEADS
GATE_RANK = 16
GATE_TAU = 16.0
D_FF = 256 * ((8 * D_MODEL // 3 + 255) // 256)
N_MOD = 9
ALPHA = (2.0 * DEPTH) ** 0.25
BETA = (8.0 * DEPTH) ** -0.25
LN_EPS = 1e-5
RMS_EPS = 1e-6
SPLITS = (A_WIDTH, A_WIDTH, A_WIDTH,
          B_KEY_DIM, B_KEY_DIM, B_VAL_DIM,
          B_VAL_DIM, GATE_RANK,
          D_MODEL, D_MODEL)
SPLIT_POINTS = tuple(int(v) for v in np.cumsum(SPLITS)[:-1])
W_IN_COLS = sum(SPLITS)

kernel_name = "hybrid_chunk_attn_gla_macaron_deepnorm_adaln"


def layer_norm(x, g, b):
    xf = x.astype(jnp.float32)
    mu = jnp.mean(xf, axis=-1, keepdims=True)
    var = jnp.mean(jnp.square(xf - mu), axis=-1, keepdims=True)
    y = (xf - mu) * lax.rsqrt(var + LN_EPS)
    return (y * g.astype(jnp.float32) + b.astype(jnp.float32)).astype(x.dtype)


def modulate(x, shift, scale):
    return x * (1.0 + scale[:, None, :]) + shift[:, None, :]


def swiglu(u, w_in, w_out):
    a, b = jnp.split(u @ w_in, 2, axis=-1)
    return (jax.nn.silu(a) * b) @ w_out


def chunk_band_attention(q, k, v, rel_bias):
    bn, s, h, dh = q.shape
    nc = s // CHUNK
    pad = A_PAST_CHUNKS * CHUNK
    kp = jnp.pad(k, ((0, 0), (pad, 0), (0, 0), (0, 0)))
    vp = jnp.pad(v, ((0, 0), (pad, 0), (0, 0), (0, 0)))
    qi = jnp.arange(CHUNK)[:, None]
    ks = jnp.arange(A_BAND)[None, :]
    rel = ks - pad - qi
    idx = jnp.clip(rel, -REL_CLIP, CHUNK - 1) + REL_CLIP
    bias = rel_bias[:, idx].astype(jnp.float32)
    qc = q.reshape(bn, nc, CHUNK, h, dh).swapaxes(0, 1)
    scale = dh ** -0.5

    def one_chunk(args):
        n, qn = args
        start = n * CHUNK
        kb = lax.dynamic_slice_in_dim(kp, start, A_BAND, axis=1)
        vb = lax.dynamic_slice_in_dim(vp, start, A_BAND, axis=1)
        sc = jnp.einsum('bqhd,bkhd->bhqk', qn, kb).astype(jnp.float32) * scale + bias
        valid = (start - pad + jnp.arange(A_BAND)) >= 0
        sc = jnp.where(valid, sc, -jnp.inf)
        p = jax.nn.softmax(sc, axis=-1).astype(vb.dtype)
        return jnp.einsum('bhqk,bkhd->bqhd', p, vb)

    out = lax.map(one_chunk, (jnp.arange(nc), qc))
    return out.swapaxes(0, 1).reshape(bn, s, h * dh)


def gla_chunk_readout(q, k, v, log_a):
    bn, s, h, dk = q.shape
    dv = v.shape[-1]
    nc = s // CHUNK
    qc = q.reshape(bn, nc, CHUNK, h, dk).astype(jnp.float32)
    kc = k.reshape(bn, nc, CHUNK, h, dk).astype(jnp.float32)
    vc = v.reshape(bn, nc, CHUNK, h, dv).astype(jnp.float32)
    cum = jnp.cumsum(log_a.reshape(bn, nc, CHUNK, h, dk).astype(jnp.float32), axis=2)
    last = cum[:, :, -1:]
    kdec = kc * jnp.exp(last - cum)
    u = jnp.einsum('bnchk,bnchv->bnhkv', kdec, vc)
    chunk_decay = jnp.exp(last[:, :, 0])

    def step(state, xs):
        dec, un, qn = xs
        state = dec[..., None] * state + un
        return state, jnp.einsum('bchk,bhkv->bchv', qn, state)

    s0 = jnp.zeros((bn, h, dk, dv), jnp.float32)
    _, o = lax.scan(step, s0, (chunk_decay.swapaxes(0, 1), u.swapaxes(0, 1), qc.swapaxes(0, 1)))
    return o.swapaxes(0, 1).reshape(bn, s, h, dv)


def token_mix(u, w_mix_in, rel_bias, w_alpha2, b_alpha, gla_norm_g, w_proj_a, w_proj_b, w_mix_out):
    bn, s, _ = u.shape
    qa, ka, va, qb, kb, vb, rb, lr, ga, gb = jnp.split(u @ w_mix_in, SPLIT_POINTS, axis=-1)
    ya = chunk_band_attention(qa.reshape(bn, s, A_HEADS, A_HEAD_DIM),
                              ka.reshape(bn, s, A_HEADS, A_HEAD_DIM),
                              va.reshape(bn, s, A_HEADS, A_HEAD_DIM), rel_bias)
    log_a = jax.nn.log_sigmoid((lr @ w_alpha2 + b_alpha).astype(jnp.float32)) / GATE_TAU
    ob = gla_chunk_readout(qb.reshape(bn, s, B_HEADS, B_HK) * (B_HK ** -0.5),
                           kb.reshape(bn, s, B_HEADS, B_HK),
                           vb.reshape(bn, s, B_HEADS, B_HV),
                           log_a.reshape(bn, s, B_HEADS, B_HK))
    ob = ob * lax.rsqrt(jnp.mean(jnp.square(ob), axis=-1, keepdims=True) + RMS_EPS)
    ob = (ob * gla_norm_g.astype(jnp.float32)).astype(u.dtype).reshape(bn, s, B_VAL_DIM)
    yb = ob * jax.nn.silu(rb)
    merged = jax.nn.sigmoid(ga) * (ya @ w_proj_a) + jax.nn.sigmoid(gb) * (yb @ w_proj_b)
    return merged @ w_mix_out


def setup_inputs(seed: int = 0) -> dict:
    key = jax.random.key(seed)
    ks = jax.random.split(key, 24)
    f32 = jnp.float32
    nrm = lambda k, shape, s: jax.random.normal(k, shape, f32) * s
    L, D = DEPTH, D_MODEL
    return {
        "x": nrm(ks[0], (BATCH, SEQ, D), 1.0),
        "c": nrm(ks[1], (BATCH, D), 1.0),
        "w_ada": nrm(ks[2], (L, D, N_MOD * D), 0.5 * D ** -0.5),
        "b_ada": nrm(ks[3], (L, N_MOD * D), 0.01),
        "ffn1_w_in": nrm(ks[4], (L, D, 2 * D_FF), D ** -0.5),
        "ffn1_w_out": nrm(ks[5], (L, D_FF, D), BETA * D_FF ** -0.5),
        "ln1_g": 1.0 + nrm(ks[6], (L, D), 0.02),
        "ln1_b": nrm(ks[7], (L, D), 0.02),
        "w_mix_in": nrm(ks[8], (L, D, W_IN_COLS), D ** -0.5),
        "rel_bias": nrm(ks[9], (L, A_HEADS, REL_SIZE), 0.5),
        "w_alpha2": nrm(ks[10], (L, GATE_RANK, B_KEY_DIM), GATE_RANK ** -0.5),
        "b_alpha": nrm(ks[11], (L, B_KEY_DIM), 0.1),
        "gla_norm_g": 1.0 + nrm(ks[12], (L, B_HV), 0.02),
        "w_proj_a": nrm(ks[13], (L, A_WIDTH, D), BETA * A_WIDTH ** -0.5),
        "w_proj_b": nrm(ks[14], (L, B_VAL_DIM, D), BETA * B_VAL_DIM ** -0.5),
        "w_mix_out": nrm(ks[15], (L, D, D), BETA * D ** -0.5),
        "ln2_g": 1.0 + nrm(ks[16], (L, D), 0.02),
        "ln2_b": nrm(ks[17], (L, D), 0.02),
        "ffn2_w_in": nrm(ks[18], (L, D, 2 * D_FF), D ** -0.5),
        "ffn2_w_out": nrm(ks[19], (L, D_FF, D), BETA * D_FF ** -0.5),
        "ln3_g": 1.0 + nrm(ks[20], (L, D), 0.02),
        "ln3_b": nrm(ks[21], (L, D), 0.02),
    }


def reference(x, c, w_ada, b_ada, ffn1_w_in, ffn1_w_out, ln1_g, ln1_b, w_mix_in, rel_bias,
              w_alpha2, b_alpha, gla_norm_g, w_proj_a, w_proj_b, w_mix_out, ln2_g, ln2_b,
              ffn2_w_in, ffn2_w_out, ln3_g, ln3_b):
    h = x
    for l in range(DEPTH):
        mod = jax.nn.silu(c) @ w_ada[l] + b_ada[l]
        sh1, sc1, g1, sh2, sc2, g2, sh3, sc3, g3 = jnp.split(mod, N_MOD, axis=-1)
        f1 = swiglu(modulate(h, sh1, sc1), ffn1_w_in[l], ffn1_w_out[l])
        h = layer_norm(ALPHA * h + 0.5 * g1[:, None, :] * f1, ln1_g[l], ln1_b[l])
        m = token_mix(modulate(h, sh2, sc2), w_mix_in[l], rel_bias[l], w_alpha2[l], b_alpha[l],
                      gla_norm_g[l], w_proj_a[l], w_proj_b[l], w_mix_out[l])
        h = layer_norm(ALPHA * h + g2[:, None, :] * m, ln2_g[l], ln2_b[l])
        f2 = swiglu(modulate(h, sh3, sc3), ffn2_w_in[l], ffn2_w_out[l])
        h = layer_norm(ALPHA * h + 0.5 * g3[:, None, :] * f2, ln3_g[l], ln3_b[l])
    return h
```

```python
import functools

import jax
import jax.numpy as jnp
from jax import lax
from jax.experimental import pallas as pl
from jax.experimental.pallas import tpu as pltpu

F32 = jnp.float32
BF16 = jnp.bfloat16

CHUNK = 64
A_HEADS = 16
A_HEAD_DIM = 64
A_WIDTH = A_HEADS * A_HEAD_DIM
A_PAST_CHUNKS = 8
REL_CLIP = 256
B_HEADS = 4
GATE_RANK = 16
GATE_TAU = 16.0
N_MOD = 9
LN_EPS = 1e-5
RMS_EPS = 1e-6

LANES = 128
VMEM_LIMIT_BYTES = 56 * 1024 * 1024

Q_SUB = 2 * CHUNK
K_WIN = (A_PAST_CHUNKS + 2) * CHUNK
E_LEN = 6 * LANES


def _params(*sem):
    return pltpu.CompilerParams(dimension_semantics=sem, vmem_limit_bytes=VMEM_LIMIT_BYTES)


def _silu(v):
    return v * jax.nn.sigmoid(v)


def _layer_norm(y, g, b):
    mu = jnp.mean(y, axis=-1, keepdims=True)
    d = y - mu
    var = jnp.mean(d * d, axis=-1, keepdims=True)
    return d * lax.rsqrt(var + LN_EPS) * g + b


def _mod_kernel(c_ref, w_ref, b_ref, o_ref):
    s = _silu(c_ref[...]).astype(BF16)
    o_ref[...] = jnp.dot(s, w_ref[...].astype(BF16), preferred_element_type=F32) + b_ref[...]


def _mod_call(c_pad, w_ada, b_ada, tn=1024):
    rows, d = c_pad.shape
    n = w_ada.shape[1]
    return pl.pallas_call(
        _mod_kernel,
        grid=(n // tn,),
        in_specs=[
            pl.BlockSpec((rows, d), lambda j: (0, 0)),
            pl.BlockSpec((d, tn), lambda j: (0, j)),
            pl.BlockSpec((1, tn), lambda j: (0, j)),
        ],
        out_specs=pl.BlockSpec((rows, tn), lambda j: (0, j)),
        out_shape=jax.ShapeDtypeStruct((rows, n), F32),
        compiler_params=_params("parallel"),
        name="mod",
    )(c_pad, w_ada, b_ada)


def _ffn_kernel(h_ref, mod_ref, wa_ref, wb_ref, wo_ref, g_ref, b_ref, o_ref, u_ref, *, rows, alpha):
    j = pl.program_id(1)
    sh, sc, gt = rows

    @pl.when(j == 0)
    def _():
        u_ref[...] = (h_ref[...] * (1.0 + mod_ref[0, sc:sc + 1, :]) + mod_ref[0, sh:sh + 1, :]).astype(BF16)
        o_ref[...] = jnp.zeros_like(o_ref)

    u = u_ref[...]
    a = jnp.dot(u, wa_ref[...], preferred_element_type=F32)
    b = jnp.dot(u, wb_ref[...], preferred_element_type=F32)
    g = (_silu(a) * b).astype(BF16)
    o_ref[...] += jnp.dot(g, wo_ref[...], preferred_element_type=F32)

    @pl.when(j == pl.num_programs(1) - 1)
    def _():
        y = alpha * h_ref[...] + (0.5 * mod_ref[0, gt:gt + 1, :]) * o_ref[...]
        o_ref[...] = _layer_norm(y, g_ref[...], b_ref[...])


def _ffn_call(h, mod, w_in, w_out, ln_g, ln_b, rows, alpha, seq, tm=512, tf=512):
    n, d = h.shape
    dff = w_out.shape[0]
    nj = dff // tf
    per_batch = seq // tm
    return pl.pallas_call(
        functools.partial(_ffn_kernel, rows=rows, alpha=alpha),
        grid=(n // tm, nj),
        in_specs=[
            pl.BlockSpec((tm, d), lambda i, j: (i, 0)),
            pl.BlockSpec((1, N_MOD, d), lambda i, j: (i // per_batch, 0, 0)),
            pl.BlockSpec((d, tf), lambda i, j: (0, j)),
            pl.BlockSpec((d, tf), lambda i, j: (0, j + nj)),
            pl.BlockSpec((tf, d), lambda i, j: (j, 0)),
            pl.BlockSpec((1, d), lambda i, j: (0, 0)),
            pl.BlockSpec((1, d), lambda i, j: (0, 0)),
        ],
        out_specs=pl.BlockSpec((tm, d), lambda i, j: (i, 0)),
        out_shape=jax.ShapeDtypeStruct((n, d), F32),
        scratch_shapes=[pltpu.VMEM((tm, d), BF16)],
        compiler_params=_params("parallel", "arbitrary"),
        name="ffn",
    )(h, mod, w_in, w_in, w_out, ln_g, ln_b)


def _mix_in_kernel(h_ref, mod_ref, w_ref, wlr_ref, p_ref, lr_ref, u_ref):
    @pl.when(pl.program_id(1) == 0)
    def _():
        u = (h_ref[...] * (1.0 + mod_ref[0, 4:5, :]) + mod_ref[0, 3:4, :]).astype(BF16)
        u_ref[...] = u
        lr_ref[...] = jnp.dot(u, wlr_ref[...], preferred_element_type=F32)

    p_ref[...] = jnp.dot(u_ref[...], w_ref[...], preferred_element_type=F32).astype(BF16)


def _mix_in_call(h, mod, w_main, w_lr, seq, tm=1024, tn=1024):
    n, d = h.shape
    cols = w_main.shape[1]
    per_batch = seq // tm
    return pl.pallas_call(
        _mix_in_kernel,
        grid=(n // tm, cols // tn),
        in_specs=[
            pl.BlockSpec((tm, d), lambda i, j: (i, 0)),
            pl.BlockSpec((1, N_MOD, d), lambda i, j: (i // per_batch, 0, 0)),
            pl.BlockSpec((d, tn), lambda i, j: (0, j)),
            pl.BlockSpec((d, LANES), lambda i, j: (0, 0)),
        ],
        out_specs=[
            pl.BlockSpec((tm, tn), lambda i, j: (i, j)),
            pl.BlockSpec((tm, LANES), lambda i, j: (i, 0)),
        ],
        out_shape=[
            jax.ShapeDtypeStruct((n, cols), BF16),
            jax.ShapeDtypeStruct((n, LANES), F32),
        ],
        scratch_shapes=[pltpu.VMEM((tm, d), BF16)],
        compiler_params=_params("parallel", "arbitrary"),
        name="mix_in",
    )(h, mod, w_main, w_lr)


def _attn_kernel(q_ref, k_ref, v_ref, e_ref, o_ref, t_ref, *, seq):
    row = lax.broadcasted_iota(jnp.int32, (Q_SUB, K_WIN), 0)
    col = lax.broadcasted_iota(jnp.int32, (Q_SUB, K_WIN), 1)
    q_chunk = row // CHUNK
    k_chunk = col // CHUNK
    in_band = (k_chunk >= q_chunk) & (k_chunk <= q_chunk + A_PAST_CHUNKS)
    for h in range(2):
        e = jnp.broadcast_to(e_ref[h], (Q_SUB, E_LEN))
        t = pltpu.roll(e, E_LEN - (Q_SUB - 1), 1, stride=1, stride_axis=0)[:, :K_WIN]
        t_ref[h] = jnp.where(in_band, t, -jnp.inf)

    lane = lax.broadcasted_iota(jnp.int32, (Q_SUB, 2 * A_HEAD_DIM), 1)
    head0 = lane < A_HEAD_DIM

    def sub_tile(q_start, k_start, width, t_start):
        q = q_ref[pl.ds(q_start, Q_SUB), :].astype(F32) * (A_HEAD_DIM ** -0.5)
        kw = k_ref[pl.ds(k_start, width), :]
        vw = v_ref[pl.ds(k_start, width), :]
        outs = []
        for h in range(2):
            qh = jnp.where(head0 if h == 0 else jnp.logical_not(head0), q, 0.0).astype(BF16)
            s = lax.dot_general(qh, kw, (((1,), (1,)), ((), ())), preferred_element_type=F32)
            s = s + t_ref[h, :, t_start:t_start + width]
            m = jnp.max(s, axis=-1, keepdims=True)
            p = jnp.exp(s - m)
            l = jnp.sum(p, axis=-1, keepdims=True)
            o = jnp.dot(p.astype(BF16), vw, preferred_element_type=F32)
            outs.append(o / l)
        o_ref[pl.ds(q_start, Q_SUB), :] = jnp.where(head0, outs[0], outs[1]).astype(o_ref.dtype)

    n_head = (K_WIN - Q_SUB) // Q_SUB
    for j in range(n_head):
        sub_tile(j * Q_SUB, 0, (j + 1) * Q_SUB, K_WIN - (j + 1) * Q_SUB)

    def body(j, carry):
        q_start = pl.multiple_of(j * Q_SUB, Q_SUB)
        k_start = pl.multiple_of(j * Q_SUB - (K_WIN - Q_SUB), Q_SUB)
        sub_tile(q_start, k_start, K_WIN, 0)
        return carry

    lax.fori_loop(n_head, seq // Q_SUB, body, 0)


def _attn_call(p, e_ext, batch, seq):
    n = p.shape[0]
    pairs = A_HEADS // 2
    w = 2 * A_HEAD_DIM
    return pl.pallas_call(
        functools.partial(_attn_kernel, seq=seq),
        grid=(batch, pairs),
        in_specs=[
            pl.BlockSpec((seq, w), lambda b, hp: (b, hp)),
            pl.BlockSpec((seq, w), lambda b, hp: (b, pairs + hp)),
            pl.BlockSpec((seq, w), lambda b, hp: (b, 2 * pairs + hp)),
            pl.BlockSpec((2, 1, E_LEN), lambda b, hp: (hp, 0, 0)),
        ],
        out_specs=pl.BlockSpec((seq, w), lambda b, hp: (b, hp)),
        out_shape=jax.ShapeDtypeStruct((n, A_WIDTH), BF16),
        scratch_shapes=[pltpu.VMEM((2, Q_SUB, K_WIN), F32)],
        compiler_params=_params("parallel", "parallel"),
        name="attn",
    )(p, p, p, e_ext)


def _gla_kernel(q_ref, k_ref, v_ref, r_ref, lr_ref, wa_ref, ba_ref, gn_ref, o_ref, st_ref, *, chunks, hk, hv):
    @pl.when(pl.program_id(1) == 0)
    def _():
        st_ref[...] = jnp.zeros_like(st_ref)

    ri = lax.broadcasted_iota(jnp.int32, (CHUNK, CHUNK), 0)
    ci = lax.broadcasted_iota(jnp.int32, (CHUNK, CHUNK), 1)
    tri = (ci <= ri).astype(BF16)

    for n in range(chunks):
        r0 = n * CHUNK
        z = jnp.dot(lr_ref[r0:r0 + CHUNK, :].astype(BF16), wa_ref[...], preferred_element_type=F32) + ba_ref[...]
        log_a = -(jnp.maximum(-z, 0.0) + jnp.log1p(jnp.exp(-jnp.abs(z)))) / GATE_TAU
        x0 = log_a.astype(BF16)
        r1 = log_a - x0.astype(F32)
        x1 = r1.astype(BF16)
        x2 = (r1 - x1.astype(F32)).astype(BF16)
        cum = (jnp.dot(tri, x0, preferred_element_type=F32)
               + jnp.dot(tri, x1, preferred_element_type=F32)
               + jnp.dot(tri, x2, preferred_element_type=F32))
        last = cum[CHUNK - 1:CHUNK, :]
        kdec = (k_ref[r0:r0 + CHUNK, :].astype(F32) * jnp.exp(last - cum)).astype(BF16)
        decay = jnp.exp(last)
        for h in range(B_HEADS):
            ks = slice(h * hk, (h + 1) * hk)
            vs = slice(h * hv, (h + 1) * hv)
            upd = lax.dot_general(v_ref[r0:r0 + CHUNK, vs], kdec[:, ks], (((0,), (0,)), ((), ())),
                                  preferred_element_type=F32)
            st = st_ref[h] * decay[:, ks] + upd
            st_ref[h] = st
            o = lax.dot_general(q_ref[r0:r0 + CHUNK, ks], st.astype(BF16), (((1,), (1,)), ((), ())),
                                preferred_element_type=F32) * (hk ** -0.5)
            o = o * lax.rsqrt(jnp.mean(o * o, axis=-1, keepdims=True) + RMS_EPS) * gn_ref[...]
            o_ref[r0:r0 + CHUNK, vs] = (o * _silu(r_ref[r0:r0 + CHUNK, vs].astype(F32))).astype(o_ref.dtype)


def _gla_call(p, lr, w_alpha2, b_alpha, gn, batch, seq, kdim, vdim, rows=512):
    n = p.shape[0]
    steps = seq // rows
    hk, hv = kdim // B_HEADS, vdim // B_HEADS
    q_blk = (3 * A_WIDTH) // kdim
    v_blk = (3 * A_WIDTH + 2 * kdim) // vdim
    row_map = lambda col: (lambda b, t: (b * steps + t, col))
    return pl.pallas_call(
        functools.partial(_gla_kernel, chunks=rows // CHUNK, hk=hk, hv=hv),
        grid=(batch, steps),
        in_specs=[
            pl.BlockSpec((rows, kdim), row_map(q_blk)),
            pl.BlockSpec((rows, kdim), row_map(q_blk + 1)),
            pl.BlockSpec((rows, vdim), row_map(v_blk)),
            pl.BlockSpec((rows, vdim), row_map(v_blk + 1)),
            pl.BlockSpec((rows, LANES), row_map(0)),
            pl.BlockSpec((LANES, kdim), lambda b, t: (0, 0)),
            pl.BlockSpec((1, kdim), lambda b, t: (0, 0)),
            pl.BlockSpec((1, hv), lambda b, t: (0, 0)),
        ],
        out_specs=pl.BlockSpec((rows, vdim), row_map(0)),
        out_shape=jax.ShapeDtypeStruct((n, vdim), BF16),
        scratch_shapes=[pltpu.VMEM((B_HEADS, hv, hk), F32)],
        compiler_params=_params("parallel", "arbitrary"),
        name="gla",
    )(p, p, p, p, lr, w_alpha2, b_alpha, gn)


def _merge_kernel(ya_ref, yb_ref, ga_ref, gb_ref, h_ref, mod_ref, wpa_ref, wpb_ref, wo_ref, g_ref, b_ref, o_ref,
                  *, alpha):
    pa = jnp.dot(ya_ref[...], wpa_ref[...], preferred_element_type=F32)
    pb = jnp.dot(yb_ref[...], wpb_ref[...], preferred_element_type=F32)
    merged = jax.nn.sigmoid(ga_ref[...].astype(F32)) * pa + jax.nn.sigmoid(gb_ref[...].astype(F32)) * pb
    m = jnp.dot(merged.astype(BF16), wo_ref[...], preferred_element_type=F32)
    y = alpha * h_ref[...] + mod_ref[0, 5:6, :] * m
    o_ref[...] = _layer_norm(y, g_ref[...], b_ref[...])


def _merge_call(ya, yb, p, h, mod, w_pa, w_pb, w_out, ln_g, ln_b, alpha, seq, tm=256):
    n, d = h.shape
    per_batch = seq // tm
    ga_blk = (p.shape[1] - 2 * d) // d
    const = lambda shape: pl.BlockSpec(shape, lambda i: (0, 0), pipeline_mode=pl.Buffered(1))
    return pl.pallas_call(
        functools.partial(_merge_kernel, alpha=alpha),
        grid=(n // tm,),
        in_specs=[
            pl.BlockSpec((tm, ya.shape[1]), lambda i: (i, 0)),
            pl.BlockSpec((tm, yb.shape[1]), lambda i: (i, 0)),
            pl.BlockSpec((tm, d), lambda i: (i, ga_blk)),
            pl.BlockSpec((tm, d), lambda i: (i, ga_blk + 1)),
            pl.BlockSpec((tm, d), lambda i: (i, 0)),
            pl.BlockSpec((1, N_MOD, d), lambda i: (i // per_batch, 0, 0)),
            const(w_pa.shape),
            const(w_pb.shape),
            const(w_out.shape),
            const((1, d)),
            const((1, d)),
        ],
        out_specs=pl.BlockSpec((tm, d), lambda i: (i, 0)),
        out_shape=jax.ShapeDtypeStruct((n, d), F32),
        compiler_params=_params("parallel"),
        name="merge",
    )(ya, yb, p, p, h, mod, w_pa, w_pb, w_out, ln_g, ln_b)


def _extended_rel_bias(rel_bias):
    heads, rel_size = rel_bias.shape
    n_low = (Q_SUB - 1) + (K_WIN - Q_SUB) - REL_CLIP
    n_high = E_LEN - n_low - rel_size
    e = jnp.concatenate([
        jnp.broadcast_to(rel_bias[:, :1], (heads, n_low)),
        rel_bias,
        jnp.broadcast_to(rel_bias[:, -1:], (heads, n_high)),
    ], axis=1)
    return e.reshape(heads, 1, E_LEN).astype(F32)


def kernel(x, c, w_ada, b_ada, ffn1_w_in, ffn1_w_out, ln1_g, ln1_b, w_mix_in, rel_bias, w_alpha2, b_alpha,
           gla_norm_g, w_proj_a, w_proj_b, w_mix_out, ln2_g, ln2_b, ffn2_w_in, ffn2_w_out, ln3_g, ln3_b):
    batch, seq, d = x.shape
    depth = w_ada.shape[0]
    alpha = (2.0 * depth) ** 0.25
    kdim = w_alpha2.shape[2]
    vdim = w_proj_b.shape[1]
    n_main = 3 * A_WIDTH + 2 * kdim + 2 * vdim
    row = lambda v: v.reshape(1, -1)

    h = x.reshape(batch * seq, d)
    c_pad = jnp.zeros((16, d), F32).at[:batch].set(c)
    for l in range(depth):
        mod = _mod_call(c_pad, w_ada[l], row(b_ada[l]))[:batch].reshape(batch, N_MOD, d)

        h = _ffn_call(h, mod, ffn1_w_in[l].astype(BF16), ffn1_w_out[l].astype(BF16), row(ln1_g[l]), row(ln1_b[l]),
                      (0, 1, 2), alpha, seq)

        w_in = w_mix_in[l]
        w_main = jnp.concatenate([w_in[:, :n_main], w_in[:, n_main + GATE_RANK:]], axis=1).astype(BF16)
        w_lr = jnp.zeros((d, LANES), BF16).at[:, :GATE_RANK].set(w_in[:, n_main:n_main + GATE_RANK].astype(BF16))
        p, lr = _mix_in_call(h, mod, w_main, w_lr, seq)

        ya = _attn_call(p, _extended_rel_bias(rel_bias[l]), batch, seq)
        w_a2 = jnp.zeros((LANES, kdim), BF16).at[:GATE_RANK].set(w_alpha2[l].astype(BF16))
        yb = _gla_call(p, lr, w_a2, row(b_alpha[l]), row(gla_norm_g[l]), batch, seq, kdim, vdim)

        h = _merge_call(ya, yb, p, h, mod, w_proj_a[l].astype(BF16), w_proj_b[l].astype(BF16),
                        w_mix_out[l].astype(BF16), row(ln2_g[l]), row(ln2_b[l]), alpha, seq)

        h = _ffn_call(h, mod, ffn2_w_in[l].astype(BF16), ffn2_w_out[l].astype(BF16), row(ln3_g[l]), row(ln3_b[l]),
                      (6, 7, 8), alpha, seq)
    return h.reshape(batch, seq, d)
```

```python
import functools

import jax
import jax.numpy as jnp
from jax import lax
from jax.experimental import pallas as pl
from jax.experimental.pallas import tpu as pltpu

F32 = jnp.float32
BF16 = jnp.bfloat16

CHUNK = 64
A_HEADS = 16
A_HEAD_DIM = 64
A_WIDTH = A_HEADS * A_HEAD_DIM
A_PAST_CHUNKS = 8
REL_CLIP = 256
B_HEADS = 4
GATE_RANK = 16
GATE_TAU = 16.0
N_MOD = 9
LN_EPS = 1e-5
RMS_EPS = 1e-6

LANES = 128
VMEM_LIMIT_BYTES = 56 * 1024 * 1024

FFN_OUT_CHUNKS = 2

Q_CHUNKS = 4
Q_SUB = Q_CHUNKS * CHUNK
K_WIN = (A_PAST_CHUNKS + Q_CHUNKS) * CHUNK
ATTN_UNROLL = 7
E_LEN = Q_SUB + K_WIN


def _params(*sem):
    return pltpu.CompilerParams(dimension_semantics=sem, vmem_limit_bytes=VMEM_LIMIT_BYTES)


def _silu(v):
    return v * jax.nn.sigmoid(v)


def _layer_norm(y, g, b):
    mu = jnp.mean(y, axis=-1, keepdims=True)
    d = y - mu
    var = jnp.mean(d * d, axis=-1, keepdims=True)
    return d * lax.rsqrt(var + LN_EPS) * g + b


def _mod_kernel(c_ref, w_ref, b_ref, o_ref):
    s = _silu(c_ref[...]).astype(BF16)
    o_ref[...] = jnp.dot(s, w_ref[...].astype(BF16), preferred_element_type=F32) + b_ref[...]


def _mod_call(c_pad, w_ada, b_ada, tn=1024):
    rows, d = c_pad.shape
    n = w_ada.shape[1]
    return pl.pallas_call(
        _mod_kernel,
        grid=(n // tn,),
        in_specs=[
            pl.BlockSpec((rows, d), lambda j: (0, 0)),
            pl.BlockSpec((d, tn), lambda j: (0, j)),
            pl.BlockSpec((1, tn), lambda j: (0, j)),
        ],
        out_specs=pl.BlockSpec((rows, tn), lambda j: (0, j)),
        out_shape=jax.ShapeDtypeStruct((rows, n), F32),
        compiler_params=_params("parallel"),
        name="mod",
    )(c_pad, w_ada, b_ada)


def _ffn_kernel(h_ref, mod_ref, wa_ref, wb_ref, wo_ref, g_ref, b_ref, o_ref, u_ref, *, rows, alpha, out_chunks):
    j = pl.program_id(1)
    last = pl.num_programs(1) - 1
    sh, sc, gt = rows

    def gated(u):
        a = jnp.dot(u, wa_ref[...], preferred_element_type=F32)
        b = jnp.dot(u, wb_ref[...], preferred_element_type=F32)
        return (_silu(a) * b).astype(BF16)

    @pl.when(j == 0)
    def _():
        u = (h_ref[...] * (1.0 + mod_ref[0, sc:sc + 1, :]) + mod_ref[0, sh:sh + 1, :]).astype(BF16)
        u_ref[...] = u
        o_ref[...] = jnp.dot(gated(u), wo_ref[...], preferred_element_type=F32)

    @pl.when(jnp.logical_and(j > 0, j < last))
    def _():
        o_ref[...] += jnp.dot(gated(u_ref[...]), wo_ref[...], preferred_element_type=F32)

    @pl.when(j == last)
    def _():
        g = gated(u_ref[...])
        tr = o_ref.shape[0] // out_chunks
        for r in range(out_chunks):
            rs = slice(r * tr, (r + 1) * tr)
            f = o_ref[rs, :] + jnp.dot(g[rs], wo_ref[...], preferred_element_type=F32)
            y = alpha * h_ref[rs, :] + (0.5 * mod_ref[0, gt:gt + 1, :]) * f
            o_ref[rs, :] = _layer_norm(y, g_ref[...], b_ref[...])


def _ffn_call(h, mod, w_in, w_out, ln_g, ln_b, rows, alpha, seq, tm=512, tf=512):
    n, d = h.shape
    dff = w_out.shape[0]
    nj = dff // tf
    per_batch = seq // tm
    return pl.pallas_call(
        functools.partial(_ffn_kernel, rows=rows, alpha=alpha, out_chunks=FFN_OUT_CHUNKS),
        grid=(n // tm, nj),
        in_specs=[
            pl.BlockSpec((tm, d), lambda i, j: (i, 0)),
            pl.BlockSpec((1, N_MOD, d), lambda i, j: (i // per_batch, 0, 0)),
            pl.BlockSpec((d, tf), lambda i, j: (0, j)),
            pl.BlockSpec((d, tf), lambda i, j: (0, j + nj)),
            pl.BlockSpec((tf, d), lambda i, j: (j, 0)),
            pl.BlockSpec((1, d), lambda i, j: (0, 0)),
            pl.BlockSpec((1, d), lambda i, j: (0, 0)),
        ],
        out_specs=pl.BlockSpec((tm, d), lambda i, j: (i, 0)),
        out_shape=jax.ShapeDtypeStruct((n, d), F32),
        scratch_shapes=[pltpu.VMEM((tm, d), BF16)],
        compiler_params=_params("parallel", "arbitrary"),
        name="ffn",
    )(h, mod, w_in, w_in, w_out, ln_g, ln_b)


def _mix_in_kernel(h_ref, mod_ref, w_ref, wlr_ref, p_ref, lr_ref, u_ref):
    j = pl.program_id(1)

    @pl.when(j == 0)
    def _():
        u = (h_ref[...] * (1.0 + mod_ref[0, 4:5, :]) + mod_ref[0, 3:4, :]).astype(BF16)
        u_ref[...] = u
        p_ref[...] = jnp.dot(u, w_ref[...], preferred_element_type=F32).astype(BF16)
        lr_ref[...] = jnp.dot(u, wlr_ref[...], preferred_element_type=F32)

    @pl.when(j > 0)
    def _():
        p_ref[...] = jnp.dot(u_ref[...], w_ref[...], preferred_element_type=F32).astype(BF16)


def _mix_in_call(h, mod, w_main, w_lr, seq, tm=1024, tn=1024):
    n, d = h.shape
    cols = w_main.shape[1]
    per_batch = seq // tm
    return pl.pallas_call(
        _mix_in_kernel,
        grid=(n // tm, cols // tn),
        in_specs=[
            pl.BlockSpec((tm, d), lambda i, j: (i, 0)),
            pl.BlockSpec((1, N_MOD, d), lambda i, j: (i // per_batch, 0, 0)),
            pl.BlockSpec((d, tn), lambda i, j: (0, j)),
            pl.BlockSpec((d, LANES), lambda i, j: (0, 0)),
        ],
        out_specs=[
            pl.BlockSpec((tm, tn), lambda i, j: (i, j)),
            pl.BlockSpec((tm, LANES), lambda i, j: (i, 0)),
        ],
        out_shape=[
            jax.ShapeDtypeStruct((n, cols), BF16),
            jax.ShapeDtypeStruct((n, LANES), F32),
        ],
        scratch_shapes=[pltpu.VMEM((tm, d), BF16)],
        compiler_params=_params("parallel", "arbitrary"),
        name="mix_in",
    )(h, mod, w_main, w_lr)


def _attn_kernel(q_ref, k_ref, v_ref, e_ref, o_ref, t_ref, *, seq):
    row = lax.broadcasted_iota(jnp.int32, (Q_SUB, K_WIN), 0)
    col = lax.broadcasted_iota(jnp.int32, (Q_SUB, K_WIN), 1)
    q_chunk = row // CHUNK
    k_chunk = col // CHUNK
    in_band = (k_chunk >= q_chunk) & (k_chunk <= q_chunk + A_PAST_CHUNKS)
    for h in range(2):
        e = jnp.broadcast_to(e_ref[h], (Q_SUB, E_LEN))
        t = pltpu.roll(e, E_LEN - (Q_SUB - 1), 1, stride=1, stride_axis=0)[:, :K_WIN]
        t_ref[h * Q_SUB:(h + 1) * Q_SUB, :] = jnp.where(in_band, t, -jnp.inf)

    lane = lax.broadcasted_iota(jnp.int32, (Q_SUB, 2 * A_HEAD_DIM), 1)
    head0 = lane < A_HEAD_DIM

    def sub_tile(q_start, k_start, width, t_start):
        q = q_ref[pl.ds(q_start, Q_SUB), :].astype(F32) * (A_HEAD_DIM ** -0.5)
        q2 = jnp.concatenate([jnp.where(head0, q, 0.0), jnp.where(head0, 0.0, q)], axis=0).astype(BF16)
        kw = k_ref[pl.ds(k_start, width), :]
        vw = v_ref[pl.ds(k_start, width), :]
        s = lax.dot_general(q2, kw, (((1,), (1,)), ((), ())), preferred_element_type=F32)
        s = s + t_ref[:, t_start:t_start + width]
        m = jnp.max(s, axis=-1, keepdims=True)
        p = jnp.exp(s - m)
        l = jnp.sum(p, axis=-1, keepdims=True)
        o = jnp.dot(p.astype(BF16), vw, preferred_element_type=F32) / l
        o_ref[pl.ds(q_start, Q_SUB), :] = jnp.where(head0, o[:Q_SUB], o[Q_SUB:]).astype(o_ref.dtype)

    n_head = (K_WIN - Q_SUB) // Q_SUB
    for j in range(n_head):
        sub_tile(j * Q_SUB, 0, (j + 1) * Q_SUB, K_WIN - (j + 1) * Q_SUB)

    def body(j, carry):
        q_start = pl.multiple_of(j * Q_SUB, Q_SUB)
        k_start = pl.multiple_of(j * Q_SUB - (K_WIN - Q_SUB), Q_SUB)
        sub_tile(q_start, k_start, K_WIN, 0)
        return carry

    lax.fori_loop(n_head, seq // Q_SUB, body, 0, unroll=ATTN_UNROLL)


def _attn_call(p, e_ext, batch, seq):
    n = p.shape[0]
    pairs = A_HEADS // 2
    w = 2 * A_HEAD_DIM
    return pl.pallas_call(
        functools.partial(_attn_kernel, seq=seq),
        grid=(batch, pairs),
        in_specs=[
            pl.BlockSpec((seq, w), lambda b, hp: (b, hp)),
            pl.BlockSpec((seq, w), lambda b, hp: (b, pairs + hp)),
            pl.BlockSpec((seq, w), lambda b, hp: (b, 2 * pairs + hp)),
            pl.BlockSpec((2, 1, E_LEN), lambda b, hp: (hp, 0, 0)),
        ],
        out_specs=pl.BlockSpec((seq, w), lambda b, hp: (b, hp)),
        out_shape=jax.ShapeDtypeStruct((n, A_WIDTH), BF16),
        scratch_shapes=[pltpu.VMEM((2 * Q_SUB, K_WIN), F32)],
        compiler_params=_params("parallel", "parallel"),
        name="attn",
    )(p, p, p, e_ext)


def _gla_kernel(q_ref, k_ref, v_ref, r_ref, lr_ref, wa_ref, ba_ref, gn_ref, o_ref, st_ref, *, chunks, hk, hv):
    @pl.when(pl.program_id(1) == 0)
    def _():
        st_ref[...] = jnp.zeros_like(st_ref)

    ri = lax.broadcasted_iota(jnp.int32, (CHUNK, CHUNK), 0)
    ci = lax.broadcasted_iota(jnp.int32, (CHUNK, CHUNK), 1)
    tri = (ci <= ri).astype(BF16)

    for n in range(chunks):
        r0 = n * CHUNK
        z = jnp.dot(lr_ref[r0:r0 + CHUNK, :].astype(BF16), wa_ref[...], preferred_element_type=F32) + ba_ref[...]
        log_a = -(jnp.maximum(-z, 0.0) + jnp.log(1.0 + jnp.exp(-jnp.abs(z)))) / GATE_TAU
        x0 = log_a.astype(BF16)
        r1 = log_a - x0.astype(F32)
        x1 = r1.astype(BF16)
        x2 = (r1 - x1.astype(F32)).astype(BF16)
        cum = (jnp.dot(tri, x0, preferred_element_type=F32)
               + jnp.dot(tri, x1, preferred_element_type=F32)
               + jnp.dot(tri, x2, preferred_element_type=F32))
        last = cum[CHUNK - 1:CHUNK, :]
        kdec = (k_ref[r0:r0 + CHUNK, :].astype(F32) * jnp.exp(last - cum)).astype(BF16)
        decay = jnp.exp(last)
        for h in range(B_HEADS):
            ks = slice(h * hk, (h + 1) * hk)
            vs = slice(h * hv, (h + 1) * hv)
            upd = lax.dot_general(v_ref[r0:r0 + CHUNK, vs], kdec[:, ks], (((0,), (0,)), ((), ())),
                                  preferred_element_type=F32)
            st = st_ref[h] * decay[:, ks] + upd
            st_ref[h] = st
            o = lax.dot_general(q_ref[r0:r0 + CHUNK, ks], st.astype(BF16), (((1,), (1,)), ((), ())),
                                preferred_element_type=F32) * (hk ** -0.5)
            o = o * lax.rsqrt(jnp.mean(o * o, axis=-1, keepdims=True) + RMS_EPS) * gn_ref[...]
            o_ref[r0:r0 + CHUNK, vs] = (o * _silu(r_ref[r0:r0 + CHUNK, vs].astype(F32))).astype(o_ref.dtype)


def _gla_call(p, lr, w_alpha2, b_alpha, gn, batch, seq, kdim, vdim, rows=512):
    n = p.shape[0]
    steps = seq // rows
    hk, hv = kdim // B_HEADS, vdim // B_HEADS
    q_blk = (3 * A_WIDTH) // kdim
    v_blk = (3 * A_WIDTH + 2 * kdim) // vdim
    row_map = lambda col: (lambda b, t: (b * steps + t, col))
    return pl.pallas_call(
        functools.partial(_gla_kernel, chunks=rows // CHUNK, hk=hk, hv=hv),
        grid=(batch, steps),
        in_specs=[
            pl.BlockSpec((rows, kdim), row_map(q_blk)),
            pl.BlockSpec((rows, kdim), row_map(q_blk + 1)),
            pl.BlockSpec((rows, vdim), row_map(v_blk)),
            pl.BlockSpec((rows, vdim), row_map(v_blk + 1)),
            pl.BlockSpec((rows, LANES), row_map(0)),
            pl.BlockSpec((LANES, kdim), lambda b, t: (0, 0)),
            pl.BlockSpec((1, kdim), lambda b, t: (0, 0)),
            pl.BlockSpec((1, hv), lambda b, t: (0, 0)),
        ],
        out_specs=pl.BlockSpec((rows, vdim), row_map(0)),
        out_shape=jax.ShapeDtypeStruct((n, vdim), BF16),
        scratch_shapes=[pltpu.VMEM((B_HEADS, hv, hk), F32)],
        compiler_params=_params("parallel", "arbitrary"),
        name="gla",
    )(p, p, p, p, lr, w_alpha2, b_alpha, gn)


def _merge_kernel(ya_ref, yb_ref, ga_ref, gb_ref, h_ref, mod_ref, wpa_ref, wpb_ref, wo_ref, g_ref, b_ref, o_ref,
                  *, alpha):
    pa = jnp.dot(ya_ref[...], wpa_ref[...], preferred_element_type=F32)
    pb = jnp.dot(yb_ref[...], wpb_ref[...], preferred_element_type=F32)
    merged = jax.nn.sigmoid(ga_ref[...].astype(F32)) * pa + jax.nn.sigmoid(gb_ref[...].astype(F32)) * pb
    m = jnp.dot(merged.astype(BF16), wo_ref[...], preferred_element_type=F32)
    y = alpha * h_ref[...] + mod_ref[0, 5:6, :] * m
    o_ref[...] = _layer_norm(y, g_ref[...], b_ref[...])


def _merge_call(ya, yb, p, h, mod, w_pa, w_pb, w_out, ln_g, ln_b, alpha, seq, tm=256):
    n, d = h.shape
    per_batch = seq // tm
    ga_blk = (p.shape[1] - 2 * d) // d
    const = lambda shape: pl.BlockSpec(shape, lambda i: (0, 0), pipeline_mode=pl.Buffered(1))
    return pl.pallas_call(
        functools.partial(_merge_kernel, alpha=alpha),
        grid=(n // tm,),
        in_specs=[
            pl.BlockSpec((tm, ya.shape[1]), lambda i: (i, 0)),
            pl.BlockSpec((tm, yb.shape[1]), lambda i: (i, 0)),
            pl.BlockSpec((tm, d), lambda i: (i, ga_blk)),
            pl.BlockSpec((tm, d), lambda i: (i, ga_blk + 1)),
            pl.BlockSpec((tm, d), lambda i: (i, 0)),
            pl.BlockSpec((1, N_MOD, d), lambda i: (i // per_batch, 0, 0)),
            const(w_pa.shape),
            const(w_pb.shape),
            const(w_out.shape),
            const((1, d)),
            const((1, d)),
        ],
        out_specs=pl.BlockSpec((tm, d), lambda i: (i, 0)),
        out_shape=jax.ShapeDtypeStruct((n, d), F32),
        compiler_params=_params("parallel"),
        name="merge",
    )(ya, yb, p, p, h, mod, w_pa, w_pb, w_out, ln_g, ln_b)


def _extended_rel_bias(rel_bias):
    heads, rel_size = rel_bias.shape
    n_low = (Q_SUB - 1) + (K_WIN - Q_SUB) - REL_CLIP
    n_high = E_LEN - n_low - rel_size
    e = jnp.concatenate([
        jnp.broadcast_to(rel_bias[:, :1], (heads, n_low)),
        rel_bias,
        jnp.broadcast_to(rel_bias[:, -1:], (heads, n_high)),
    ], axis=1)
    return e.reshape(heads, 1, E_LEN).astype(F32)


def kernel(x, c, w_ada, b_ada, ffn1_w_in, ffn1_w_out, ln1_g, ln1_b, w_mix_in, rel_bias, w_alpha2, b_alpha,
           gla_norm_g, w_proj_a, w_proj_b, w_mix_out, ln2_g, ln2_b, ffn2_w_in, ffn2_w_out, ln3_g, ln3_b):
    batch, seq, d = x.shape
    depth = w_ada.shape[0]
    alpha = (2.0 * depth) ** 0.25
    kdim = w_alpha2.shape[2]
    vdim = w_proj_b.shape[1]
    n_main = 3 * A_WIDTH + 2 * kdim + 2 * vdim
    row = lambda v: v.reshape(1, -1)

    h = x.reshape(batch * seq, d)
    c_pad = jnp.zeros((16, d), F32).at[:batch].set(c)
    for l in range(depth):
        mod = _mod_call(c_pad, w_ada[l], row(b_ada[l]))[:batch].reshape(batch, N_MOD, d)

        h = _ffn_call(h, mod, ffn1_w_in[l].astype(BF16), ffn1_w_out[l].astype(BF16), row(ln1_g[l]), row(ln1_b[l]),
                      (0, 1, 2), alpha, seq)

        w_in = w_mix_in[l]
        w_main = jnp.concatenate([w_in[:, :n_main], w_in[:, n_main + GATE_RANK:]], axis=1).astype(BF16)
        w_lr = jnp.zeros((d, LANES), BF16).at[:, :GATE_RANK].set(w_in[:, n_main:n_main + GATE_RANK].astype(BF16))
        p, lr = _mix_in_call(h, mod, w_main, w_lr, seq)

        ya = _attn_call(p, _extended_rel_bias(rel_bias[l]), batch, seq)
        w_a2 = jnp.zeros((LANES, kdim), BF16).at[:GATE_RANK].set(w_alpha2[l].astype(BF16))
        yb = _gla_call(p, lr, w_a2, row(b_alpha[l]), row(gla_norm_g[l]), batch, seq, kdim, vdim)

        h = _merge_call(ya, yb, p, h, mod, w_proj_a[l].astype(BF16), w_proj_b[l].astype(BF16),
                        w_mix_out[l].astype(BF16), row(ln2_g[l]), row(ln2_b[l]), alpha, seq)

        h = _ffn_call(h, mod, ffn2_w_in[l].astype(BF16), ffn2_w_out[l].astype(BF16), row(ln3_g[l]), row(ln3_b[l]),
                      (6, 7, 8), alpha, seq)
    return h.reshape(batch, seq, d)
```

```python
import functools

import jax
import jax.numpy as jnp
from jax import lax
from jax.experimental import pallas as pl
from jax.experimental.pallas import tpu as pltpu

F32 = jnp.float32
BF16 = jnp.bfloat16

CHUNK = 64
A_HEADS = 16
A_HEAD_DIM = 64
A_WIDTH = A_HEADS * A_HEAD_DIM
A_PAST_CHUNKS = 8
REL_CLIP = 256
B_HEADS = 4
GATE_RANK = 16
GATE_TAU = 16.0
N_MOD = 9
LN_EPS = 1e-5
RMS_EPS = 1e-6

LANES = 128
VMEM_LIMIT_BYTES = 56 * 1024 * 1024

FFN_OUT_CHUNKS = 4

Q_CHUNKS = 4
Q_SUB = Q_CHUNKS * CHUNK
K_WIN = (A_PAST_CHUNKS + Q_CHUNKS) * CHUNK
ATTN_UNROLL = 7
E_LEN = Q_SUB + K_WIN


def _params(*sem):
    return pltpu.CompilerParams(dimension_semantics=sem, vmem_limit_bytes=VMEM_LIMIT_BYTES)


def _silu(v):
    return v * jax.nn.sigmoid(v)


def _layer_norm(y, g, b):
    mu = jnp.mean(y, axis=-1, keepdims=True)
    d = y - mu
    var = jnp.mean(d * d, axis=-1, keepdims=True)
    return d * lax.rsqrt(var + LN_EPS) * g + b


def _mod_kernel(c_ref, w_ref, b_ref, o_ref):
    s = _silu(c_ref[...]).astype(BF16)
    o_ref[...] = jnp.dot(s, w_ref[...].astype(BF16), preferred_element_type=F32) + b_ref[...]


def _mod_call(c_pad, w_ada, b_ada, tn=1024):
    rows, d = c_pad.shape
    n = w_ada.shape[1]
    return pl.pallas_call(
        _mod_kernel,
        grid=(n // tn,),
        in_specs=[
            pl.BlockSpec((rows, d), lambda j: (0, 0)),
            pl.BlockSpec((d, tn), lambda j: (0, j)),
            pl.BlockSpec((1, tn), lambda j: (0, j)),
        ],
        out_specs=pl.BlockSpec((rows, tn), lambda j: (0, j)),
        out_shape=jax.ShapeDtypeStruct((rows, n), F32),
        compiler_params=_params("parallel"),
        name="mod",
    )(c_pad, w_ada, b_ada)


def _ffn_kernel(h_ref, mod_ref, wa_ref, wb_ref, wo_ref, g_ref, b_ref, o_ref, u_ref, *, rows, alpha, out_chunks):
    j = pl.program_id(1)
    last = pl.num_programs(1) - 1
    sh, sc, gt = rows

    def gated(u):
        a = jnp.dot(u, wa_ref[...], preferred_element_type=F32)
        b = jnp.dot(u, wb_ref[...], preferred_element_type=F32)
        return (_silu(a) * b).astype(BF16)

    @pl.when(j == 0)
    def _():
        u = (h_ref[...] * (1.0 + mod_ref[0, sc:sc + 1, :]) + mod_ref[0, sh:sh + 1, :]).astype(BF16)
        u_ref[...] = u
        o_ref[...] = jnp.dot(gated(u), wo_ref[...], preferred_element_type=F32)

    @pl.when(jnp.logical_and(j > 0, j < last))
    def _():
        o_ref[...] += jnp.dot(gated(u_ref[...]), wo_ref[...], preferred_element_type=F32)

    @pl.when(j == last)
    def _():
        g = gated(u_ref[...])
        tr = o_ref.shape[0] // out_chunks
        for r in range(out_chunks):
            rs = slice(r * tr, (r + 1) * tr)
            f = o_ref[rs, :] + jnp.dot(g[rs], wo_ref[...], preferred_element_type=F32)
            y = alpha * h_ref[rs, :] + (0.5 * mod_ref[0, gt:gt + 1, :]) * f
            o_ref[rs, :] = _layer_norm(y, g_ref[...], b_ref[...])


def _ffn_call(h, mod, w_in, w_out, ln_g, ln_b, rows, alpha, seq, tm=1024, tf=512):
    n, d = h.shape
    dff = w_out.shape[0]
    nj = dff // tf
    per_batch = seq // tm
    return pl.pallas_call(
        functools.partial(_ffn_kernel, rows=rows, alpha=alpha, out_chunks=FFN_OUT_CHUNKS),
        grid=(n // tm, nj),
        in_specs=[
            pl.BlockSpec((tm, d), lambda i, j: (i, 0), pipeline_mode=pl.Buffered(1)),
            pl.BlockSpec((1, N_MOD, d), lambda i, j: (i // per_batch, 0, 0)),
            pl.BlockSpec((d, tf), lambda i, j: (0, j)),
            pl.BlockSpec((d, tf), lambda i, j: (0, j + nj)),
            pl.BlockSpec((tf, d), lambda i, j: (j, 0)),
            pl.BlockSpec((1, d), lambda i, j: (0, 0)),
            pl.BlockSpec((1, d), lambda i, j: (0, 0)),
        ],
        out_specs=pl.BlockSpec((tm, d), lambda i, j: (i, 0)),
        out_shape=jax.ShapeDtypeStruct((n, d), F32),
        scratch_shapes=[pltpu.VMEM((tm, d), BF16)],
        compiler_params=_params("parallel", "arbitrary"),
        name="ffn",
    )(h, mod, w_in, w_in, w_out, ln_g, ln_b)


def _repack_kernel(w_ref, main_ref, lr_ref, *, n_main):
    n_out = main_ref.shape[1]
    main_ref[:, :n_main] = w_ref[:, :n_main].astype(BF16)
    main_ref[:, n_main:] = w_ref[:, n_main + GATE_RANK:n_out + GATE_RANK].astype(BF16)
    lane = lax.broadcasted_iota(jnp.int32, lr_ref.shape, 1)
    lr_ref[...] = jnp.where(lane < GATE_RANK, w_ref[:, n_main:n_main + LANES], 0.0).astype(BF16)


def _repack_call(w_in, n_main, tr=256):
    d, cols = w_in.shape
    return pl.pallas_call(
        functools.partial(_repack_kernel, n_main=n_main),
        grid=(d // tr,),
        in_specs=[pl.BlockSpec((tr, cols), lambda i: (i, 0))],
        out_specs=[
            pl.BlockSpec((tr, cols - GATE_RANK), lambda i: (i, 0)),
            pl.BlockSpec((tr, LANES), lambda i: (i, 0)),
        ],
        out_shape=[
            jax.ShapeDtypeStruct((d, cols - GATE_RANK), BF16),
            jax.ShapeDtypeStruct((d, LANES), BF16),
        ],
        compiler_params=_params("parallel"),
        name="repack",
    )(w_in)


def _mix_in_kernel(h_ref, mod_ref, w_ref, wlr_ref, p_ref, lr_ref, u_ref):
    j = pl.program_id(1)

    @pl.when(j == 0)
    def _():
        u = (h_ref[...] * (1.0 + mod_ref[0, 4:5, :]) + mod_ref[0, 3:4, :]).astype(BF16)
        u_ref[...] = u
        p_ref[...] = jnp.dot(u, w_ref[...], preferred_element_type=F32).astype(BF16)
        lr_ref[...] = jnp.dot(u, wlr_ref[...], preferred_element_type=F32)

    @pl.when(j > 0)
    def _():
        p_ref[...] = jnp.dot(u_ref[...], w_ref[...], preferred_element_type=F32).astype(BF16)


def _mix_in_call(h, mod, w_main, w_lr, seq, tm=1024, tn=2048):
    n, d = h.shape
    cols = w_main.shape[1]
    per_batch = seq // tm
    return pl.pallas_call(
        _mix_in_kernel,
        grid=(n // tm, cols // tn),
        in_specs=[
            pl.BlockSpec((tm, d), lambda i, j: (i, 0)),
            pl.BlockSpec((1, N_MOD, d), lambda i, j: (i // per_batch, 0, 0)),
            pl.BlockSpec((d, tn), lambda i, j: (0, j)),
            pl.BlockSpec((d, LANES), lambda i, j: (0, 0)),
        ],
        out_specs=[
            pl.BlockSpec((tm, tn), lambda i, j: (i, j)),
            pl.BlockSpec((tm, LANES), lambda i, j: (i, 0)),
        ],
        out_shape=[
            jax.ShapeDtypeStruct((n, cols), BF16),
            jax.ShapeDtypeStruct((n, LANES), F32),
        ],
        scratch_shapes=[pltpu.VMEM((tm, d), BF16)],
        compiler_params=_params("parallel", "arbitrary"),
        name="mix_in",
    )(h, mod, w_main, w_lr)


def _attn_kernel(q_ref, k_ref, v_ref, e_ref, o_ref, t_ref, *, seq):
    row = lax.broadcasted_iota(jnp.int32, (Q_SUB, K_WIN), 0)
    col = lax.broadcasted_iota(jnp.int32, (Q_SUB, K_WIN), 1)
    q_chunk = row // CHUNK
    k_chunk = col // CHUNK
    in_band = (k_chunk >= q_chunk) & (k_chunk <= q_chunk + A_PAST_CHUNKS)
    for h in range(2):
        e = jnp.broadcast_to(e_ref[h], (Q_SUB, E_LEN))
        t = pltpu.roll(e, E_LEN - (Q_SUB - 1), 1, stride=1, stride_axis=0)[:, :K_WIN]
        t_ref[h * Q_SUB:(h + 1) * Q_SUB, :] = jnp.where(in_band, t, -jnp.inf)

    lane = lax.broadcasted_iota(jnp.int32, (Q_SUB, 2 * A_HEAD_DIM), 1)
    head0 = lane < A_HEAD_DIM

    def sub_tile(q_start, k_start, width, t_start):
        q = q_ref[pl.ds(q_start, Q_SUB), :].astype(F32) * (A_HEAD_DIM ** -0.5)
        q2 = jnp.concatenate([jnp.where(head0, q, 0.0), jnp.where(head0, 0.0, q)], axis=0).astype(BF16)
        kw = k_ref[pl.ds(k_start, width), :]
        vw = v_ref[pl.ds(k_start, width), :]
        s = lax.dot_general(q2, kw, (((1,), (1,)), ((), ())), preferred_element_type=F32)
        s = s + t_ref[:, t_start:t_start + width]
        m = jnp.max(s, axis=-1, keepdims=True)
        p = jnp.exp(s - m)
        l = jnp.sum(p, axis=-1, keepdims=True)
        o = jnp.dot(p.astype(BF16), vw, preferred_element_type=F32) / l
        o_ref[pl.ds(q_start, Q_SUB), :] = jnp.where(head0, o[:Q_SUB], o[Q_SUB:]).astype(o_ref.dtype)

    n_head = (K_WIN - Q_SUB) // Q_SUB
    for j in range(n_head):
        sub_tile(j * Q_SUB, 0, (j + 1) * Q_SUB, K_WIN - (j + 1) * Q_SUB)

    def body(j, carry):
        q_start = pl.multiple_of(j * Q_SUB, Q_SUB)
        k_start = pl.multiple_of(j * Q_SUB - (K_WIN - Q_SUB), Q_SUB)
        sub_tile(q_start, k_start, K_WIN, 0)
        return carry

    lax.fori_loop(n_head, seq // Q_SUB, body, 0, unroll=ATTN_UNROLL)


def _attn_call(p, e_ext, batch, seq):
    n = p.shape[0]
    pairs = A_HEADS // 2
    w = 2 * A_HEAD_DIM
    return pl.pallas_call(
        functools.partial(_attn_kernel, seq=seq),
        grid=(batch, pairs),
        in_specs=[
            pl.BlockSpec((seq, w), lambda b, hp: (b, hp)),
            pl.BlockSpec((seq, w), lambda b, hp: (b, pairs + hp)),
            pl.BlockSpec((seq, w), lambda b, hp: (b, 2 * pairs + hp)),
            pl.BlockSpec((2, 1, E_LEN), lambda b, hp: (hp, 0, 0)),
        ],
        out_specs=pl.BlockSpec((seq, w), lambda b, hp: (b, hp)),
        out_shape=jax.ShapeDtypeStruct((n, A_WIDTH), BF16),
        scratch_shapes=[pltpu.VMEM((2 * Q_SUB, K_WIN), F32)],
        compiler_params=_params("parallel", "parallel"),
        name="attn",
    )(p, p, p, e_ext)


def _gla_kernel(q_ref, k_ref, v_ref, r_ref, lr_ref, wa_ref, ba_ref, gn_ref, o_ref, st_ref, *, chunks, hk, hv):
    @pl.when(pl.program_id(1) == 0)
    def _():
        st_ref[...] = jnp.zeros_like(st_ref)

    ri = lax.broadcasted_iota(jnp.int32, (CHUNK, CHUNK), 0)
    ci = lax.broadcasted_iota(jnp.int32, (CHUNK, CHUNK), 1)
    tri = (ci <= ri).astype(BF16)

    for n in range(chunks):
        r0 = n * CHUNK
        z = jnp.dot(lr_ref[r0:r0 + CHUNK, :].astype(BF16), wa_ref[...], preferred_element_type=F32) + ba_ref[...]
        log_a = -(jnp.maximum(-z, 0.0) + jnp.log(1.0 + jnp.exp(-jnp.abs(z)))) / GATE_TAU
        x0 = log_a.astype(BF16)
        x1 = (log_a - x0.astype(F32)).astype(BF16)
        cum = jnp.dot(tri, x0, preferred_element_type=F32) + jnp.dot(tri, x1, preferred_element_type=F32)
        last = cum[CHUNK - 1:CHUNK, :]
        kdec = (k_ref[r0:r0 + CHUNK, :].astype(F32) * jnp.exp(last - cum)).astype(BF16)
        decay = jnp.exp(last)
        for h in range(B_HEADS):
            ks = slice(h * hk, (h + 1) * hk)
            vs = slice(h * hv, (h + 1) * hv)
            upd = lax.dot_general(v_ref[r0:r0 + CHUNK, vs], kdec[:, ks], (((0,), (0,)), ((), ())),
                                  preferred_element_type=F32)
            st = st_ref[h] * decay[:, ks] + upd
            st_ref[h] = st
            o = lax.dot_general(q_ref[r0:r0 + CHUNK, ks], st.astype(BF16), (((1,), (1,)), ((), ())),
                                preferred_element_type=F32) * (hk ** -0.5)
            o = o * lax.rsqrt(jnp.mean(o * o, axis=-1, keepdims=True) + RMS_EPS) * gn_ref[...]
            o_ref[r0:r0 + CHUNK, vs] = (o * _silu(r_ref[r0:r0 + CHUNK, vs].astype(F32))).astype(o_ref.dtype)


def _gla_call(p, lr, w_alpha2, b_alpha, gn, batch, seq, kdim, vdim, rows=512):
    n = p.shape[0]
    steps = seq // rows
    hk, hv = kdim // B_HEADS, vdim // B_HEADS
    q_blk = (3 * A_WIDTH) // kdim
    v_blk = (3 * A_WIDTH + 2 * kdim) // vdim
    row_map = lambda col: (lambda b, t: (b * steps + t, col))
    return pl.pallas_call(
        functools.partial(_gla_kernel, chunks=rows // CHUNK, hk=hk, hv=hv),
        grid=(batch, steps),
        in_specs=[
            pl.BlockSpec((rows, kdim), row_map(q_blk)),
            pl.BlockSpec((rows, kdim), row_map(q_blk + 1)),
            pl.BlockSpec((rows, vdim), row_map(v_blk)),
            pl.BlockSpec((rows, vdim), row_map(v_blk + 1)),
            pl.BlockSpec((rows, LANES), row_map(0)),
            pl.BlockSpec((LANES, kdim), lambda b, t: (0, 0)),
            pl.BlockSpec((1, kdim), lambda b, t: (0, 0)),
            pl.BlockSpec((1, hv), lambda b, t: (0, 0)),
        ],
        out_specs=pl.BlockSpec((rows, vdim), row_map(0)),
        out_shape=jax.ShapeDtypeStruct((n, vdim), BF16),
        scratch_shapes=[pltpu.VMEM((B_HEADS, hv, hk), F32)],
        compiler_params=_params("parallel", "arbitrary"),
        name="gla",
    )(p, p, p, p, lr, w_alpha2, b_alpha, gn)


def _merge_kernel(ya_ref, yb_ref, ga_ref, gb_ref, h_ref, mod_ref, wpa_ref, wpb_ref, wo_ref, g_ref, b_ref, o_ref,
                  *, alpha):
    pa = jnp.dot(ya_ref[...], wpa_ref[...], preferred_element_type=F32)
    pb = jnp.dot(yb_ref[...], wpb_ref[...], preferred_element_type=F32)
    merged = jax.nn.sigmoid(ga_ref[...].astype(F32)) * pa + jax.nn.sigmoid(gb_ref[...].astype(F32)) * pb
    m = jnp.dot(merged.astype(BF16), wo_ref[...], preferred_element_type=F32)
    y = alpha * h_ref[...] + mod_ref[0, 5:6, :] * m
    o_ref[...] = _layer_norm(y, g_ref[...], b_ref[...])


def _merge_call(ya, yb, p, h, mod, w_pa, w_pb, w_out, ln_g, ln_b, alpha, seq, tm=256):
    n, d = h.shape
    per_batch = seq // tm
    ga_blk = (p.shape[1] - 2 * d) // d
    const = lambda shape: pl.BlockSpec(shape, lambda i: (0, 0), pipeline_mode=pl.Buffered(1))
    return pl.pallas_call(
        functools.partial(_merge_kernel, alpha=alpha),
        grid=(n // tm,),
        in_specs=[
            pl.BlockSpec((tm, ya.shape[1]), lambda i: (i, 0)),
            pl.BlockSpec((tm, yb.shape[1]), lambda i: (i, 0)),
            pl.BlockSpec((tm, d), lambda i: (i, ga_blk)),
            pl.BlockSpec((tm, d), lambda i: (i, ga_blk + 1)),
            pl.BlockSpec((tm, d), lambda i: (i, 0)),
            pl.BlockSpec((1, N_MOD, d), lambda i: (i // per_batch, 0, 0)),
            const(w_pa.shape),
            const(w_pb.shape),
            const(w_out.shape),
            const((1, d)),
            const((1, d)),
        ],
        out_specs=pl.BlockSpec((tm, d), lambda i: (i, 0)),
        out_shape=jax.ShapeDtypeStruct((n, d), F32),
        compiler_params=_params("parallel"),
        name="merge",
    )(ya, yb, p, p, h, mod, w_pa, w_pb, w_out, ln_g, ln_b)


def _extended_rel_bias(rel_bias):
    heads, rel_size = rel_bias.shape
    n_low = (Q_SUB - 1) + (K_WIN - Q_SUB) - REL_CLIP
    n_high = E_LEN - n_low - rel_size
    e = jnp.concatenate([
        jnp.broadcast_to(rel_bias[:, :1], (heads, n_low)),
        rel_bias,
        jnp.broadcast_to(rel_bias[:, -1:], (heads, n_high)),
    ], axis=1)
    return e.reshape(heads, 1, E_LEN).astype(F32)


def kernel(x, c, w_ada, b_ada, ffn1_w_in, ffn1_w_out, ln1_g, ln1_b, w_mix_in, rel_bias, w_alpha2, b_alpha,
           gla_norm_g, w_proj_a, w_proj_b, w_mix_out, ln2_g, ln2_b, ffn2_w_in, ffn2_w_out, ln3_g, ln3_b):
    batch, seq, d = x.shape
    depth = w_ada.shape[0]
    alpha = (2.0 * depth) ** 0.25
    kdim = w_alpha2.shape[2]
    vdim = w_proj_b.shape[1]
    n_main = 3 * A_WIDTH + 2 * kdim + 2 * vdim
    row = lambda v: v.reshape(1, -1)

    h = x.reshape(batch * seq, d)
    c_pad = jnp.zeros((16, d), F32).at[:batch].set(c)
    for l in range(depth):
        mod = _mod_call(c_pad, w_ada[l], row(b_ada[l]))[:batch].reshape(batch, N_MOD, d)

        h = _ffn_call(h, mod, ffn1_w_in[l].astype(BF16), ffn1_w_out[l].astype(BF16), row(ln1_g[l]), row(ln1_b[l]),
                      (0, 1, 2), alpha, seq)

        w_main, w_lr = _repack_call(w_mix_in[l], n_main)
        p, lr = _mix_in_call(h, mod, w_main, w_lr, seq)

        ya = _attn_call(p, _extended_rel_bias(rel_bias[l]), batch, seq)
        w_a2 = jnp.zeros((LANES, kdim), BF16).at[:GATE_RANK].set(w_alpha2[l].astype(BF16))
        yb = _gla_call(p, lr, w_a2, row(b_alpha[l]), row(gla_norm_g[l]), batch, seq, kdim, vdim)

        h = _merge_call(ya, yb, p, h, mod, w_proj_a[l].astype(BF16), w_proj_b[l].astype(BF16),
                        w_mix_out[l].astype(BF16), row(ln2_g[l]), row(ln2_b[l]), alpha, seq)

        h = _ffn_call(h, mod, ffn2_w_in[l].astype(BF16), ffn2_w_out[l].astype(BF16), row(ln3_g[l]), row(ln3_b[l]),
                      (6, 7, 8), alpha, seq)
    return h.reshape(batch, seq, d)
```

```python
import functools

import jax
import jax.numpy as jnp
from jax import lax
from jax.experimental import pallas as pl
from jax.experimental.pallas import tpu as pltpu

F32 = jnp.float32
BF16 = jnp.bfloat16

CHUNK = 64
A_HEADS = 16
A_HEAD_DIM = 64
A_WIDTH = A_HEADS * A_HEAD_DIM
A_PAST_CHUNKS = 8
REL_CLIP = 256
B_HEADS = 4
GATE_RANK = 16
GATE_TAU = 16.0
N_MOD = 9
LN_EPS = 1e-5
RMS_EPS = 1e-6

LANES = 128
SUBLANES = 8
VMEM_LIMIT_BYTES = 56 * 1024 * 1024

FFN_OUT_CHUNKS = 4

Q_CHUNKS = 4
Q_SUB = Q_CHUNKS * CHUNK
K_WIN = (A_PAST_CHUNKS + Q_CHUNKS) * CHUNK
ATTN_UNROLL = 7
E_LEN = Q_SUB + K_WIN


def _params(*sem):
    return pltpu.CompilerParams(dimension_semantics=sem, vmem_limit_bytes=VMEM_LIMIT_BYTES)


def _silu(v):
    return v * jax.nn.sigmoid(v)


def _layer_norm(y, g, b):
    mu = jnp.mean(y, axis=-1, keepdims=True)
    d = y - mu
    var = jnp.mean(d * d, axis=-1, keepdims=True)
    return d * lax.rsqrt(var + LN_EPS) * g + b


def _mod_kernel(c_ref, w_ref, b_ref, o_ref):
    s = _silu(c_ref[...]).astype(BF16)
    o_ref[...] = jnp.dot(s, w_ref[...].astype(BF16), preferred_element_type=F32) + b_ref[...]


def _mod_call(c_pad, w_ada, b_ada, tn=1024):
    rows, d = c_pad.shape
    n = w_ada.shape[1]
    return pl.pallas_call(
        _mod_kernel,
        grid=(n // tn,),
        in_specs=[
            pl.BlockSpec((rows, d), lambda j: (0, 0)),
            pl.BlockSpec((d, tn), lambda j: (0, j)),
            pl.BlockSpec((1, tn), lambda j: (0, j)),
        ],
        out_specs=pl.BlockSpec((rows, tn), lambda j: (0, j)),
        out_shape=jax.ShapeDtypeStruct((rows, n), F32),
        compiler_params=_params("parallel"),
        name="mod",
    )(c_pad, w_ada, b_ada)


def _ffn_kernel(h_ref, mod_ref, wa_ref, wb_ref, wo_ref, g_ref, b_ref, o_ref, u_ref, *, rows, alpha, out_chunks):
    j = pl.program_id(1)
    last = pl.num_programs(1) - 1
    sh, sc, gt = rows

    def gated(u):
        a = jnp.dot(u, wa_ref[...], preferred_element_type=F32)
        b = jnp.dot(u, wb_ref[...], preferred_element_type=F32)
        return (_silu(a) * b).astype(BF16)

    @pl.when(j == 0)
    def _():
        u = (h_ref[...] * (1.0 + mod_ref[0, sc:sc + 1, :]) + mod_ref[0, sh:sh + 1, :]).astype(BF16)
        u_ref[...] = u
        o_ref[...] = jnp.dot(gated(u), wo_ref[...], preferred_element_type=F32)

    @pl.when(jnp.logical_and(j > 0, j < last))
    def _():
        o_ref[...] += jnp.dot(gated(u_ref[...]), wo_ref[...], preferred_element_type=F32)

    @pl.when(j == last)
    def _():
        g = gated(u_ref[...])
        tr = o_ref.shape[0] // out_chunks
        for r in range(out_chunks):
            rs = slice(r * tr, (r + 1) * tr)
            f = o_ref[rs, :] + jnp.dot(g[rs], wo_ref[...], preferred_element_type=F32)
            y = alpha * h_ref[rs, :] + (0.5 * mod_ref[0, gt:gt + 1, :]) * f
            o_ref[rs, :] = _layer_norm(y, g_ref[...], b_ref[...])


def _ffn_call(h, mod, w_in, w_out, ln_g, ln_b, rows, alpha, seq, tm=1024, tf=512):
    n, d = h.shape
    dff = w_out.shape[0]
    nj = dff // tf
    per_batch = seq // tm
    return pl.pallas_call(
        functools.partial(_ffn_kernel, rows=rows, alpha=alpha, out_chunks=FFN_OUT_CHUNKS),
        grid=(n // tm, nj),
        in_specs=[
            pl.BlockSpec((tm, d), lambda i, j: (i, 0)),
            pl.BlockSpec((1, N_MOD, d), lambda i, j: (i // per_batch, 0, 0)),
            pl.BlockSpec((d, tf), lambda i, j: (0, j)),
            pl.BlockSpec((d, tf), lambda i, j: (0, j + nj)),
            pl.BlockSpec((tf, d), lambda i, j: (j, 0)),
            pl.BlockSpec((1, d), lambda i, j: (0, 0)),
            pl.BlockSpec((1, d), lambda i, j: (0, 0)),
        ],
        out_specs=pl.BlockSpec((tm, d), lambda i, j: (i, 0)),
        out_shape=jax.ShapeDtypeStruct((n, d), F32),
        scratch_shapes=[pltpu.VMEM((tm, d), BF16)],
        compiler_params=_params("parallel", "arbitrary"),
        name="ffn",
    )(h, mod, w_in, w_in, w_out, ln_g, ln_b)


def _repack_kernel(w_ref, wlr_ref, main_ref, lr_ref):
    main_ref[...] = w_ref[...].astype(BF16)
    row = lax.broadcasted_iota(jnp.int32, lr_ref.shape, 0)
    lr_ref[...] = jnp.where(row < GATE_RANK, wlr_ref[...], 0.0).astype(BF16)


def _repack_call(w_t, n_main, tr=1024):
    rows, d = w_t.shape
    main_blocks = n_main // tr
    src = lambda k: (SUBLANES * (k * (tr // SUBLANES) + jnp.where(k < main_blocks, 0, GATE_RANK // SUBLANES)), 0)
    return pl.pallas_call(
        _repack_kernel,
        grid=((rows - GATE_RANK) // tr,),
        in_specs=[
            pl.BlockSpec((pl.Element(tr), pl.Element(d)), src),
            pl.BlockSpec((pl.Element(LANES), pl.Element(d)), lambda k: (n_main, 0)),
        ],
        out_specs=[
            pl.BlockSpec((tr, d), lambda k: (k, 0)),
            pl.BlockSpec((LANES, d), lambda k: (0, 0)),
        ],
        out_shape=[
            jax.ShapeDtypeStruct((rows - GATE_RANK, d), BF16),
            jax.ShapeDtypeStruct((LANES, d), BF16),
        ],
        compiler_params=_params("arbitrary"),
        name="repack",
    )(w_t, w_t)


_NT = (((1,), (1,)), ((), ()))


def _mix_in_kernel(h_ref, mod_ref, w_ref, wlr_ref, p_ref, lr_ref, u_ref):
    j = pl.program_id(1)

    @pl.when(j == 0)
    def _():
        u = (h_ref[...] * (1.0 + mod_ref[0, 4:5, :]) + mod_ref[0, 3:4, :]).astype(BF16)
        u_ref[...] = u
        p_ref[...] = lax.dot_general(u, w_ref[...], _NT, preferred_element_type=F32).astype(BF16)
        lr_ref[...] = lax.dot_general(u, wlr_ref[...], _NT, preferred_element_type=F32)

    @pl.when(j > 0)
    def _():
        p_ref[...] = lax.dot_general(u_ref[...], w_ref[...], _NT, preferred_element_type=F32).astype(BF16)


def _mix_in_call(h, mod, w_main_t, w_lr_t, seq, tm=1024, tn=2048):
    n, d = h.shape
    cols = w_main_t.shape[0]
    per_batch = seq // tm
    return pl.pallas_call(
        _mix_in_kernel,
        grid=(n // tm, cols // tn),
        in_specs=[
            pl.BlockSpec((tm, d), lambda i, j: (i, 0)),
            pl.BlockSpec((1, N_MOD, d), lambda i, j: (i // per_batch, 0, 0)),
            pl.BlockSpec((tn, d), lambda i, j: (j, 0)),
            pl.BlockSpec((LANES, d), lambda i, j: (0, 0)),
        ],
        out_specs=[
            pl.BlockSpec((tm, tn), lambda i, j: (i, j)),
            pl.BlockSpec((tm, LANES), lambda i, j: (i, 0)),
        ],
        out_shape=[
            jax.ShapeDtypeStruct((n, cols), BF16),
            jax.ShapeDtypeStruct((n, LANES), F32),
        ],
        scratch_shapes=[pltpu.VMEM((tm, d), BF16)],
        compiler_params=_params("parallel", "arbitrary"),
        name="mix_in",
    )(h, mod, w_main_t, w_lr_t)


def _attn_kernel(q_ref, k_ref, v_ref, e_ref, o_ref, t_ref, *, seq):
    row = lax.broadcasted_iota(jnp.int32, (Q_SUB, K_WIN), 0)
    col = lax.broadcasted_iota(jnp.int32, (Q_SUB, K_WIN), 1)
    q_chunk = row // CHUNK
    k_chunk = col // CHUNK
    in_band = (k_chunk >= q_chunk) & (k_chunk <= q_chunk + A_PAST_CHUNKS)
    for h in range(2):
        e = jnp.broadcast_to(e_ref[h], (Q_SUB, E_LEN))
        t = pltpu.roll(e, E_LEN - (Q_SUB - 1), 1, stride=1, stride_axis=0)[:, :K_WIN]
        t_ref[h * Q_SUB:(h + 1) * Q_SUB, :] = jnp.where(in_band, t, -jnp.inf)

    lane = lax.broadcasted_iota(jnp.int32, (Q_SUB, 2 * A_HEAD_DIM), 1)
    head0 = lane < A_HEAD_DIM

    def sub_tile(q_start, k_start, width, t_start):
        q = q_ref[pl.ds(q_start, Q_SUB), :].astype(F32) * (A_HEAD_DIM ** -0.5)
        q2 = jnp.concatenate([jnp.where(head0, q, 0.0), jnp.where(head0, 0.0, q)], axis=0).astype(BF16)
        kw = k_ref[pl.ds(k_start, width), :]
        vw = v_ref[pl.ds(k_start, width), :]
        s = lax.dot_general(q2, kw, (((1,), (1,)), ((), ())), preferred_element_type=F32)
        s = s + t_ref[:, t_start:t_start + width]
        m = jnp.max(s, axis=-1, keepdims=True)
        p = jnp.exp(s - m)
        l = jnp.sum(p, axis=-1, keepdims=True)
        o = jnp.dot(p.astype(BF16), vw, preferred_element_type=F32) / l
        o_ref[pl.ds(q_start, Q_SUB), :] = jnp.where(head0, o[:Q_SUB], o[Q_SUB:]).astype(o_ref.dtype)

    n_head = (K_WIN - Q_SUB) // Q_SUB
    for j in range(n_head):
        sub_tile(j * Q_SUB, 0, (j + 1) * Q_SUB, K_WIN - (j + 1) * Q_SUB)

    def body(j, carry):
        q_start = pl.multiple_of(j * Q_SUB, Q_SUB)
        k_start = pl.multiple_of(j * Q_SUB - (K_WIN - Q_SUB), Q_SUB)
        sub_tile(q_start, k_start, K_WIN, 0)
        return carry

    lax.fori_loop(n_head, seq // Q_SUB, body, 0, unroll=ATTN_UNROLL)


def _attn_call(p, e_ext, batch, seq):
    n = p.shape[0]
    pairs = A_HEADS // 2
    w = 2 * A_HEAD_DIM
    return pl.pallas_call(
        functools.partial(_attn_kernel, seq=seq),
        grid=(batch, pairs),
        in_specs=[
            pl.BlockSpec((seq, w), lambda b, hp: (b, hp)),
            pl.BlockSpec((seq, w), lambda b, hp: (b, pairs + hp)),
            pl.BlockSpec((seq, w), lambda b, hp: (b, 2 * pairs + hp)),
            pl.BlockSpec((2, 1, E_LEN), lambda b, hp: (hp, 0, 0)),
        ],
        out_specs=pl.BlockSpec((seq, w), lambda b, hp: (b, hp)),
        out_shape=jax.ShapeDtypeStruct((n, A_WIDTH), BF16),
        scratch_shapes=[pltpu.VMEM((2 * Q_SUB, K_WIN), F32)],
        compiler_params=_params("parallel", "parallel"),
        name="attn",
    )(p, p, p, e_ext)


def _gla_kernel(q_ref, k_ref, v_ref, r_ref, lr_ref, wa_ref, ba_ref, gn_ref, o_ref, st_ref, *, chunks, hk, hv):
    @pl.when(pl.program_id(1) == 0)
    def _():
        st_ref[...] = jnp.zeros_like(st_ref)

    ri = lax.broadcasted_iota(jnp.int32, (CHUNK, CHUNK), 0)
    ci = lax.broadcasted_iota(jnp.int32, (CHUNK, CHUNK), 1)
    tri = (ci <= ri).astype(BF16)

    for n in range(chunks):
        r0 = n * CHUNK
        z = jnp.dot(lr_ref[r0:r0 + CHUNK, :].astype(BF16), wa_ref[...], preferred_element_type=F32) + ba_ref[...]
        log_a = -(jnp.maximum(-z, 0.0) + jnp.log(1.0 + jnp.exp(-jnp.abs(z)))) / GATE_TAU
        x0 = log_a.astype(BF16)
        x1 = (log_a - x0.astype(F32)).astype(BF16)
        cum = jnp.dot(tri, x0, preferred_element_type=F32) + jnp.dot(tri, x1, preferred_element_type=F32)
        last = cum[CHUNK - 1:CHUNK, :]
        kdec = (k_ref[r0:r0 + CHUNK, :].astype(F32) * jnp.exp(last - cum)).astype(BF16)
        decay = jnp.exp(last)
        for h in range(B_HEADS):
            ks = slice(h * hk, (h + 1) * hk)
            vs = slice(h * hv, (h + 1) * hv)
            upd = lax.dot_general(v_ref[r0:r0 + CHUNK, vs], kdec[:, ks], (((0,), (0,)), ((), ())),
                                  preferred_element_type=F32)
            st = st_ref[h] * decay[:, ks] + upd
            st_ref[h] = st
            o = lax.dot_general(q_ref[r0:r0 + CHUNK, ks], st.astype(BF16), (((1,), (1,)), ((), ())),
                                preferred_element_type=F32) * (hk ** -0.5)
            o = o * lax.rsqrt(jnp.mean(o * o, axis=-1, keepdims=True) + RMS_EPS) * gn_ref[...]
            o_ref[r0:r0 + CHUNK, vs] = (o * _silu(r_ref[r0:r0 + CHUNK, vs].astype(F32))).astype(o_ref.dtype)


def _gla_call(p, lr, w_alpha2, b_alpha, gn, batch, seq, kdim, vdim, rows=512):
    n = p.shape[0]
    steps = seq // rows
    hk, hv = kdim // B_HEADS, vdim // B_HEADS
    q_blk = (3 * A_WIDTH) // kdim
    v_blk = (3 * A_WIDTH + 2 * kdim) // vdim
    row_map = lambda col: (lambda b, t: (b * steps + t, col))
    return pl.pallas_call(
        functools.partial(_gla_kernel, chunks=rows // CHUNK, hk=hk, hv=hv),
        grid=(batch, steps),
        in_specs=[
            pl.BlockSpec((rows, kdim), row_map(q_blk)),
            pl.BlockSpec((rows, kdim), row_map(q_blk + 1)),
            pl.BlockSpec((rows, vdim), row_map(v_blk)),
            pl.BlockSpec((rows, vdim), row_map(v_blk + 1)),
            pl.BlockSpec((rows, LANES), row_map(0)),
            pl.BlockSpec((LANES, kdim), lambda b, t: (0, 0)),
            pl.BlockSpec((1, kdim), lambda b, t: (0, 0)),
            pl.BlockSpec((1, hv), lambda b, t: (0, 0)),
        ],
        out_specs=pl.BlockSpec((rows, vdim), row_map(0)),
        out_shape=jax.ShapeDtypeStruct((n, vdim), BF16),
        scratch_shapes=[pltpu.VMEM((B_HEADS, hv, hk), F32)],
        compiler_params=_params("parallel", "arbitrary"),
        name="gla",
    )(p, p, p, p, lr, w_alpha2, b_alpha, gn)


def _merge_kernel(ya_ref, yb_ref, ga_ref, gb_ref, h_ref, mod_ref, wpa_ref, wpb_ref, wo_ref, g_ref, b_ref, o_ref,
                  *, alpha):
    pa = jnp.dot(ya_ref[...], wpa_ref[...], preferred_element_type=F32)
    pb = jnp.dot(yb_ref[...], wpb_ref[...], preferred_element_type=F32)
    merged = jax.nn.sigmoid(ga_ref[...].astype(F32)) * pa + jax.nn.sigmoid(gb_ref[...].astype(F32)) * pb
    m = jnp.dot(merged.astype(BF16), wo_ref[...], preferred_element_type=F32)
    y = alpha * h_ref[...] + mod_ref[0, 5:6, :] * m
    o_ref[...] = _layer_norm(y, g_ref[...], b_ref[...])


def _merge_call(ya, yb, p, h, mod, w_pa, w_pb, w_out, ln_g, ln_b, alpha, seq, tm=256):
    n, d = h.shape
    per_batch = seq // tm
    ga_blk = (p.shape[1] - 2 * d) // d
    const = lambda shape: pl.BlockSpec(shape, lambda i: (0, 0), pipeline_mode=pl.Buffered(1))
    return pl.pallas_call(
        functools.partial(_merge_kernel, alpha=alpha),
        grid=(n // tm,),
        in_specs=[
            pl.BlockSpec((tm, ya.shape[1]), lambda i: (i, 0)),
            pl.BlockSpec((tm, yb.shape[1]), lambda i: (i, 0)),
            pl.BlockSpec((tm, d), lambda i: (i, ga_blk)),
            pl.BlockSpec((tm, d), lambda i: (i, ga_blk + 1)),
            pl.BlockSpec((tm, d), lambda i: (i, 0)),
            pl.BlockSpec((1, N_MOD, d), lambda i: (i // per_batch, 0, 0)),
            const(w_pa.shape),
            const(w_pb.shape),
            const(w_out.shape),
            const((1, d)),
            const((1, d)),
        ],
        out_specs=pl.BlockSpec((tm, d), lambda i: (i, 0)),
        out_shape=jax.ShapeDtypeStruct((n, d), F32),
        compiler_params=_params("parallel"),
        name="merge",
    )(ya, yb, p, p, h, mod, w_pa, w_pb, w_out, ln_g, ln_b)


def _extended_rel_bias(rel_bias):
    heads, rel_size = rel_bias.shape
    n_low = (Q_SUB - 1) + (K_WIN - Q_SUB) - REL_CLIP
    n_high = E_LEN - n_low - rel_size
    e = jnp.concatenate([
        jnp.broadcast_to(rel_bias[:, :1], (heads, n_low)),
        rel_bias,
        jnp.broadcast_to(rel_bias[:, -1:], (heads, n_high)),
    ], axis=1)
    return e.reshape(heads, 1, E_LEN).astype(F32)


def kernel(x, c, w_ada, b_ada, ffn1_w_in, ffn1_w_out, ln1_g, ln1_b, w_mix_in, rel_bias, w_alpha2, b_alpha,
           gla_norm_g, w_proj_a, w_proj_b, w_mix_out, ln2_g, ln2_b, ffn2_w_in, ffn2_w_out, ln3_g, ln3_b):
    batch, seq, d = x.shape
    depth = w_ada.shape[0]
    alpha = (2.0 * depth) ** 0.25
    kdim = w_alpha2.shape[2]
    vdim = w_proj_b.shape[1]
    n_main = 3 * A_WIDTH + 2 * kdim + 2 * vdim
    row = lambda v: v.reshape(1, -1)

    h = x.reshape(batch * seq, d)
    c_pad = jnp.zeros((16, d), F32).at[:batch].set(c)
    for l in range(depth):
        mod = _mod_call(c_pad, w_ada[l], row(b_ada[l]))[:batch].reshape(batch, N_MOD, d)

        h = _ffn_call(h, mod, ffn1_w_in[l].astype(BF16), ffn1_w_out[l].astype(BF16), row(ln1_g[l]), row(ln1_b[l]),
                      (0, 1, 2), alpha, seq)

        w_main_t, w_lr_t = _repack_call(jnp.swapaxes(w_mix_in, 1, 2)[l], n_main)
        p, lr = _mix_in_call(h, mod, w_main_t, w_lr_t, seq)

        ya = _attn_call(p, _extended_rel_bias(rel_bias[l]), batch, seq)
        w_a2 = jnp.zeros((LANES, kdim), BF16).at[:GATE_RANK].set(w_alpha2[l].astype(BF16))
        yb = _gla_call(p, lr, w_a2, row(b_alpha[l]), row(gla_norm_g[l]), batch, seq, kdim, vdim)

        h = _merge_call(ya, yb, p, h, mod, w_proj_a[l].astype(BF16), w_proj_b[l].astype(BF16),
                        w_mix_out[l].astype(BF16), row(ln2_g[l]), row(ln2_b[l]), alpha, seq)

        h = _ffn_call(h, mod, ffn2_w_in[l].astype(BF16), ffn2_w_out[l].astype(BF16), row(ln3_g[l]), row(ln3_b[l]),
                      (6, 7, 8), alpha, seq)
    return h.reshape(batch, seq, d)
```

```python
import functools

import jax
import jax.numpy as jnp
from jax import lax
from jax.experimental import pallas as pl
from jax.experimental.pallas import tpu as pltpu

F32 = jnp.float32
BF16 = jnp.bfloat16

CHUNK = 64
A_HEADS = 16
A_HEAD_DIM = 64
A_WIDTH = A_HEADS * A_HEAD_DIM
A_PAST_CHUNKS = 8
REL_CLIP = 256
B_HEADS = 4
GATE_RANK = 16
GATE_TAU = 16.0
N_MOD = 9
LN_EPS = 1e-5
RMS_EPS = 1e-6

LANES = 128
SUBLANES = 8
VMEM_LIMIT_BYTES = 56 * 1024 * 1024

FFN_OUT_CHUNKS = 4

Q_CHUNKS = 4
Q_SUB = Q_CHUNKS * CHUNK
K_WIN = (A_PAST_CHUNKS + Q_CHUNKS) * CHUNK
ATTN_UNROLL = 7
E_LEN = Q_SUB + K_WIN


def _params(*sem):
    return pltpu.CompilerParams(dimension_semantics=sem, vmem_limit_bytes=VMEM_LIMIT_BYTES)


def _silu(v):
    return v * jax.nn.sigmoid(v)


def _layer_norm(y, g, b):
    mu = jnp.mean(y, axis=-1, keepdims=True)
    d = y - mu
    var = jnp.mean(d * d, axis=-1, keepdims=True)
    return d * lax.rsqrt(var + LN_EPS) * g + b


def _mod_kernel(c_ref, w_ref, b_ref, o_ref):
    s = _silu(c_ref[...]).astype(BF16)
    o_ref[...] = jnp.dot(s, w_ref[...].astype(BF16), preferred_element_type=F32) + b_ref[...]


def _mod_call(c_pad, w_ada, b_ada, tn=1024):
    rows, d = c_pad.shape
    n = w_ada.shape[1]
    return pl.pallas_call(
        _mod_kernel,
        grid=(n // tn,),
        in_specs=[
            pl.BlockSpec((rows, d), lambda j: (0, 0)),
            pl.BlockSpec((d, tn), lambda j: (0, j)),
            pl.BlockSpec((1, tn), lambda j: (0, j)),
        ],
        out_specs=pl.BlockSpec((rows, tn), lambda j: (0, j)),
        out_shape=jax.ShapeDtypeStruct((rows, n), F32),
        compiler_params=_params("parallel"),
        name="mod",
    )(c_pad, w_ada, b_ada)


def _ffn_kernel(h_ref, mod_ref, wa_ref, wb_ref, wo_ref, g_ref, b_ref, o_ref, u_ref, *, rows, alpha, out_chunks):
    j = pl.program_id(1)
    last = pl.num_programs(1) - 1
    sh, sc, gt = rows

    def gated(u):
        a = jnp.dot(u, wa_ref[...], preferred_element_type=F32)
        b = jnp.dot(u, wb_ref[...], preferred_element_type=F32)
        return (_silu(a) * b).astype(BF16)

    @pl.when(j == 0)
    def _():
        u = (h_ref[...] * (1.0 + mod_ref[0, sc:sc + 1, :]) + mod_ref[0, sh:sh + 1, :]).astype(BF16)
        u_ref[...] = u
        o_ref[...] = jnp.dot(gated(u), wo_ref[...], preferred_element_type=F32)

    @pl.when(jnp.logical_and(j > 0, j < last))
    def _():
        o_ref[...] += jnp.dot(gated(u_ref[...]), wo_ref[...], preferred_element_type=F32)

    @pl.when(j == last)
    def _():
        g = gated(u_ref[...])
        tr = o_ref.shape[0] // out_chunks
        for r in range(out_chunks):
            rs = slice(r * tr, (r + 1) * tr)
            f = o_ref[rs, :] + jnp.dot(g[rs], wo_ref[...], preferred_element_type=F32)
            y = alpha * h_ref[rs, :] + (0.5 * mod_ref[0, gt:gt + 1, :]) * f
            o_ref[rs, :] = _layer_norm(y, g_ref[...], b_ref[...])


def _ffn_call(h, mod, w_in, w_out, ln_g, ln_b, rows, alpha, seq, tm=1024, tf=512):
    n, d = h.shape
    dff = w_out.shape[0]
    nj = dff // tf
    per_batch = seq // tm
    return pl.pallas_call(
        functools.partial(_ffn_kernel, rows=rows, alpha=alpha, out_chunks=FFN_OUT_CHUNKS),
        grid=(n // tm, nj),
        in_specs=[
            pl.BlockSpec((tm, d), lambda i, j: (i, 0)),
            pl.BlockSpec((1, N_MOD, d), lambda i, j: (i // per_batch, 0, 0)),
            pl.BlockSpec((d, tf), lambda i, j: (0, j)),
            pl.BlockSpec((d, tf), lambda i, j: (0, j + nj)),
            pl.BlockSpec((tf, d), lambda i, j: (j, 0)),
            pl.BlockSpec((1, d), lambda i, j: (0, 0)),
            pl.BlockSpec((1, d), lambda i, j: (0, 0)),
        ],
        out_specs=pl.BlockSpec((tm, d), lambda i, j: (i, 0)),
        out_shape=jax.ShapeDtypeStruct((n, d), F32),
        scratch_shapes=[pltpu.VMEM((tm, d), BF16)],
        compiler_params=_params("parallel", "arbitrary"),
        name="ffn",
    )(h, mod, w_in, w_in, w_out, ln_g, ln_b)


def _repack_kernel(w_ref, wlr_ref, main_ref, lr_ref):
    main_ref[...] = w_ref[...].astype(BF16)
    row = lax.broadcasted_iota(jnp.int32, lr_ref.shape, 0)
    lr_ref[...] = jnp.where(row < GATE_RANK, wlr_ref[...], 0.0).astype(BF16)


def _repack_call(w_t, n_main, tr=1024):
    rows, d = w_t.shape
    main_blocks = n_main // tr
    src = lambda k: (SUBLANES * (k * (tr // SUBLANES) + jnp.where(k < main_blocks, 0, GATE_RANK // SUBLANES)), 0)
    return pl.pallas_call(
        _repack_kernel,
        grid=((rows - GATE_RANK) // tr,),
        in_specs=[
            pl.BlockSpec((pl.Element(tr), pl.Element(d)), src),
            pl.BlockSpec((pl.Element(LANES), pl.Element(d)), lambda k: (n_main, 0)),
        ],
        out_specs=[
            pl.BlockSpec((tr, d), lambda k: (k, 0)),
            pl.BlockSpec((LANES, d), lambda k: (0, 0)),
        ],
        out_shape=[
            jax.ShapeDtypeStruct((rows - GATE_RANK, d), BF16),
            jax.ShapeDtypeStruct((LANES, d), BF16),
        ],
        compiler_params=_params("arbitrary"),
        name="repack",
    )(w_t, w_t)


_NT = (((1,), (1,)), ((), ()))


def _mix_in_kernel(h_ref, mod_ref, w_ref, wlr_ref, p_ref, lr_ref, u_ref):
    j = pl.program_id(1)

    @pl.when(j == 0)
    def _():
        u = (h_ref[...] * (1.0 + mod_ref[0, 4:5, :]) + mod_ref[0, 3:4, :]).astype(BF16)
        u_ref[...] = u
        p_ref[...] = lax.dot_general(u, w_ref[...], _NT, preferred_element_type=F32).astype(BF16)
        lr_ref[...] = lax.dot_general(u, wlr_ref[...], _NT, preferred_element_type=F32)

    @pl.when(j > 0)
    def _():
        p_ref[...] = lax.dot_general(u_ref[...], w_ref[...], _NT, preferred_element_type=F32).astype(BF16)


def _mix_in_call(h, mod, w_main_t, w_lr_t, seq, tm=1024, tn=2048):
    n, d = h.shape
    cols = w_main_t.shape[0]
    per_batch = seq // tm
    return pl.pallas_call(
        _mix_in_kernel,
        grid=(n // tm, cols // tn),
        in_specs=[
            pl.BlockSpec((tm, d), lambda i, j: (i, 0)),
            pl.BlockSpec((1, N_MOD, d), lambda i, j: (i // per_batch, 0, 0)),
            pl.BlockSpec((tn, d), lambda i, j: (j, 0)),
            pl.BlockSpec((LANES, d), lambda i, j: (0, 0)),
        ],
        out_specs=[
            pl.BlockSpec((tm, tn), lambda i, j: (i, j)),
            pl.BlockSpec((tm, LANES), lambda i, j: (i, 0)),
        ],
        out_shape=[
            jax.ShapeDtypeStruct((n, cols), BF16),
            jax.ShapeDtypeStruct((n, LANES), F32),
        ],
        scratch_shapes=[pltpu.VMEM((tm, d), BF16)],
        compiler_params=_params("parallel", "arbitrary"),
        name="mix_in",
    )(h, mod, w_main_t, w_lr_t)


def _bias_table_kernel(e_ref, t_ref):
    row = lax.broadcasted_iota(jnp.int32, (Q_SUB, K_WIN), 0)
    col = lax.broadcasted_iota(jnp.int32, (Q_SUB, K_WIN), 1)
    q_chunk = row // CHUNK
    k_chunk = col // CHUNK
    in_band = (k_chunk >= q_chunk) & (k_chunk <= q_chunk + A_PAST_CHUNKS)
    for h in range(2):
        e = jnp.broadcast_to(e_ref[h], (Q_SUB, E_LEN))
        t = pltpu.roll(e, E_LEN - (Q_SUB - 1), 1, stride=1, stride_axis=0)[:, :K_WIN]
        t_ref[0, h * Q_SUB:(h + 1) * Q_SUB, :] = jnp.where(in_band, t, -jnp.inf)


def _bias_table_call(e_ext):
    pairs = e_ext.shape[0] // 2
    return pl.pallas_call(
        _bias_table_kernel,
        grid=(pairs,),
        in_specs=[pl.BlockSpec((2, 1, E_LEN), lambda hp: (hp, 0, 0))],
        out_specs=pl.BlockSpec((1, 2 * Q_SUB, K_WIN), lambda hp: (hp, 0, 0)),
        out_shape=jax.ShapeDtypeStruct((pairs, 2 * Q_SUB, K_WIN), F32),
        compiler_params=_params("parallel"),
        name="bias_table",
    )(e_ext)


def _attn_kernel(q_ref, k_ref, v_ref, t_ref, o_ref, *, seq):
    lane = lax.broadcasted_iota(jnp.int32, (Q_SUB, 2 * A_HEAD_DIM), 1)
    head0 = lane < A_HEAD_DIM

    def sub_tile(q_start, k_start, width, t_start):
        q = q_ref[pl.ds(q_start, Q_SUB), :].astype(F32) * (A_HEAD_DIM ** -0.5)
        q2 = jnp.concatenate([jnp.where(head0, q, 0.0), jnp.where(head0, 0.0, q)], axis=0).astype(BF16)
        kw = k_ref[pl.ds(k_start, width), :]
        vw = v_ref[pl.ds(k_start, width), :]
        s = lax.dot_general(q2, kw, (((1,), (1,)), ((), ())), preferred_element_type=F32)
        s = s + t_ref[0, :, t_start:t_start + width]
        m = jnp.max(s, axis=-1, keepdims=True)
        p = jnp.exp(s - m)
        l = jnp.sum(p, axis=-1, keepdims=True)
        o = jnp.dot(p.astype(BF16), vw, preferred_element_type=F32) / l
        o_ref[pl.ds(q_start, Q_SUB), :] = jnp.where(head0, o[:Q_SUB], o[Q_SUB:]).astype(o_ref.dtype)

    n_head = (K_WIN - Q_SUB) // Q_SUB
    for j in range(n_head):
        sub_tile(j * Q_SUB, 0, (j + 1) * Q_SUB, K_WIN - (j + 1) * Q_SUB)

    def body(j, carry):
        q_start = pl.multiple_of(j * Q_SUB, Q_SUB)
        k_start = pl.multiple_of(j * Q_SUB - (K_WIN - Q_SUB), Q_SUB)
        sub_tile(q_start, k_start, K_WIN, 0)
        return carry

    lax.fori_loop(n_head, seq // Q_SUB, body, 0, unroll=ATTN_UNROLL)


def _attn_call(p, table, batch, seq):
    n = p.shape[0]
    pairs = A_HEADS // 2
    w = 2 * A_HEAD_DIM
    return pl.pallas_call(
        functools.partial(_attn_kernel, seq=seq),
        grid=(batch, pairs),
        in_specs=[
            pl.BlockSpec((seq, w), lambda b, hp: (b, hp)),
            pl.BlockSpec((seq, w), lambda b, hp: (b, pairs + hp)),
            pl.BlockSpec((seq, w), lambda b, hp: (b, 2 * pairs + hp)),
            pl.BlockSpec((1, 2 * Q_SUB, K_WIN), lambda b, hp: (hp, 0, 0)),
        ],
        out_specs=pl.BlockSpec((seq, w), lambda b, hp: (b, hp)),
        out_shape=jax.ShapeDtypeStruct((n, A_WIDTH), BF16),
        compiler_params=_params("parallel", "parallel"),
        name="attn",
    )(p, p, p, table)


def _gla_kernel(q_ref, k_ref, v_ref, r_ref, lr_ref, wa_ref, ba_ref, gn_ref, o_ref, st_ref, *, chunks, hk, hv):
    @pl.when(pl.program_id(1) == 0)
    def _():
        st_ref[...] = jnp.zeros_like(st_ref)

    ri = lax.broadcasted_iota(jnp.int32, (CHUNK, CHUNK), 0)
    ci = lax.broadcasted_iota(jnp.int32, (CHUNK, CHUNK), 1)
    tri = (ci <= ri).astype(BF16)

    state = [st_ref[h] for h in range(B_HEADS)]
    for n in range(chunks):
        r0 = n * CHUNK
        z = jnp.dot(lr_ref[r0:r0 + CHUNK, :].astype(BF16), wa_ref[...], preferred_element_type=F32) + ba_ref[...]
        log_a = -(jnp.maximum(-z, 0.0) + jnp.log(1.0 + jnp.exp(-jnp.abs(z)))) / GATE_TAU
        x0 = log_a.astype(BF16)
        x1 = (log_a - x0.astype(F32)).astype(BF16)
        cum = jnp.dot(tri, x0, preferred_element_type=F32) + jnp.dot(tri, x1, preferred_element_type=F32)
        last = cum[CHUNK - 1:CHUNK, :]
        kdec = (k_ref[r0:r0 + CHUNK, :].astype(F32) * jnp.exp(last - cum)).astype(BF16)
        decay = jnp.exp(last)
        for h in range(B_HEADS):
            ks = slice(h * hk, (h + 1) * hk)
            vs = slice(h * hv, (h + 1) * hv)
            upd = lax.dot_general(v_ref[r0:r0 + CHUNK, vs], kdec[:, ks], (((0,), (0,)), ((), ())),
                                  preferred_element_type=F32)
            st = state[h] * decay[:, ks] + upd
            state[h] = st
            o = lax.dot_general(q_ref[r0:r0 + CHUNK, ks], st.astype(BF16), (((1,), (1,)), ((), ())),
                                preferred_element_type=F32)
            o = o * lax.rsqrt(jnp.mean(o * o, axis=-1, keepdims=True) + RMS_EPS * hk) * gn_ref[...]
            o_ref[r0:r0 + CHUNK, vs] = (o * _silu(r_ref[r0:r0 + CHUNK, vs].astype(F32))).astype(o_ref.dtype)

    for h in range(B_HEADS):
        st_ref[h] = state[h]


def _gla_call(p, lr, w_alpha2, b_alpha, gn, batch, seq, kdim, vdim, rows=512):
    n = p.shape[0]
    steps = seq // rows
    hk, hv = kdim // B_HEADS, vdim // B_HEADS
    q_blk = (3 * A_WIDTH) // kdim
    v_blk = (3 * A_WIDTH + 2 * kdim) // vdim
    row_map = lambda col: (lambda b, t: (b * steps + t, col))
    return pl.pallas_call(
        functools.partial(_gla_kernel, chunks=rows // CHUNK, hk=hk, hv=hv),
        grid=(batch, steps),
        in_specs=[
            pl.BlockSpec((rows, kdim), row_map(q_blk)),
            pl.BlockSpec((rows, kdim), row_map(q_blk + 1)),
            pl.BlockSpec((rows, vdim), row_map(v_blk)),
            pl.BlockSpec((rows, vdim), row_map(v_blk + 1)),
            pl.BlockSpec((rows, LANES), row_map(0)),
            pl.BlockSpec((LANES, kdim), lambda b, t: (0, 0)),
            pl.BlockSpec((1, kdim), lambda b, t: (0, 0)),
            pl.BlockSpec((1, hv), lambda b, t: (0, 0)),
        ],
        out_specs=pl.BlockSpec((rows, vdim), row_map(0)),
        out_shape=jax.ShapeDtypeStruct((n, vdim), BF16),
        scratch_shapes=[pltpu.VMEM((B_HEADS, hv, hk), F32)],
        compiler_params=_params("parallel", "arbitrary"),
        name="gla",
    )(p, p, p, p, lr, w_alpha2, b_alpha, gn)


def _merge_kernel(ya_ref, yb_ref, ga_ref, gb_ref, h_ref, mod_ref, wpa_ref, wpb_ref, wo_ref, g_ref, b_ref, o_ref,
                  *, alpha, chunks):
    tr = o_ref.shape[0] // chunks
    for r in range(chunks):
        rs = slice(r * tr, (r + 1) * tr)
        pa = jnp.dot(ya_ref[rs, :], wpa_ref[...], preferred_element_type=F32)
        pb = jnp.dot(yb_ref[rs, :], wpb_ref[...], preferred_element_type=F32)
        merged = (jax.nn.sigmoid(ga_ref[rs, :].astype(F32)) * pa
                  + jax.nn.sigmoid(gb_ref[rs, :].astype(F32)) * pb)
        m = jnp.dot(merged.astype(BF16), wo_ref[...], preferred_element_type=F32)
        y = alpha * h_ref[rs, :] + mod_ref[0, 5:6, :] * m
        o_ref[rs, :] = _layer_norm(y, g_ref[...], b_ref[...])


def _merge_call(ya, yb, p, h, mod, w_pa, w_pb, w_out, ln_g, ln_b, alpha, seq, tm=512, chunks=2):
    n, d = h.shape
    per_batch = seq // tm
    ga_blk = (p.shape[1] - 2 * d) // d
    const = lambda shape: pl.BlockSpec(shape, lambda i: (0, 0), pipeline_mode=pl.Buffered(1))
    return pl.pallas_call(
        functools.partial(_merge_kernel, alpha=alpha, chunks=chunks),
        grid=(n // tm,),
        in_specs=[
            pl.BlockSpec((tm, ya.shape[1]), lambda i: (i, 0)),
            pl.BlockSpec((tm, yb.shape[1]), lambda i: (i, 0)),
            pl.BlockSpec((tm, d), lambda i: (i, ga_blk)),
            pl.BlockSpec((tm, d), lambda i: (i, ga_blk + 1)),
            pl.BlockSpec((tm, d), lambda i: (i, 0)),
            pl.BlockSpec((1, N_MOD, d), lambda i: (i // per_batch, 0, 0)),
            const(w_pa.shape),
            const(w_pb.shape),
            const(w_out.shape),
            const((1, d)),
            const((1, d)),
        ],
        out_specs=pl.BlockSpec((tm, d), lambda i: (i, 0)),
        out_shape=jax.ShapeDtypeStruct((n, d), F32),
        compiler_params=_params("parallel"),
        name="merge",
    )(ya, yb, p, p, h, mod, w_pa, w_pb, w_out, ln_g, ln_b)


def _extended_rel_bias(rel_bias):
    heads, rel_size = rel_bias.shape
    n_low = (Q_SUB - 1) + (K_WIN - Q_SUB) - REL_CLIP
    n_high = E_LEN - n_low - rel_size
    e = jnp.concatenate([
        jnp.broadcast_to(rel_bias[:, :1], (heads, n_low)),
        rel_bias,
        jnp.broadcast_to(rel_bias[:, -1:], (heads, n_high)),
    ], axis=1)
    return e.reshape(heads, 1, E_LEN).astype(F32)


def kernel(x, c, w_ada, b_ada, ffn1_w_in, ffn1_w_out, ln1_g, ln1_b, w_mix_in, rel_bias, w_alpha2, b_alpha,
           gla_norm_g, w_proj_a, w_proj_b, w_mix_out, ln2_g, ln2_b, ffn2_w_in, ffn2_w_out, ln3_g, ln3_b):
    batch, seq, d = x.shape
    depth = w_ada.shape[0]
    alpha = (2.0 * depth) ** 0.25
    kdim = w_alpha2.shape[2]
    vdim = w_proj_b.shape[1]
    n_main = 3 * A_WIDTH + 2 * kdim + 2 * vdim
    row = lambda v: v.reshape(1, -1)

    h = x.reshape(batch * seq, d)
    c_pad = jnp.zeros((16, d), F32).at[:batch].set(c)
    for l in range(depth):
        mod = _mod_call(c_pad, w_ada[l], row(b_ada[l]))[:batch].reshape(batch, N_MOD, d)

        h = _ffn_call(h, mod, ffn1_w_in[l].astype(BF16), ffn1_w_out[l].astype(BF16), row(ln1_g[l]), row(ln1_b[l]),
                      (0, 1, 2), alpha, seq)

        w_main_t, w_lr_t = _repack_call(jnp.swapaxes(w_mix_in, 1, 2)[l], n_main)
        p, lr = _mix_in_call(h, mod, w_main_t, w_lr_t, seq)

        ya = _attn_call(p, _bias_table_call(_extended_rel_bias(rel_bias[l])), batch, seq)
        w_a2 = jnp.zeros((LANES, kdim), BF16).at[:GATE_RANK].set(w_alpha2[l].astype(BF16))
        yb = _gla_call(p, lr, w_a2, row(b_alpha[l]), row(gla_norm_g[l]), batch, seq, kdim, vdim)

        h = _merge_call(ya, yb, p, h, mod, w_proj_a[l].astype(BF16), w_proj_b[l].astype(BF16),
                        w_mix_out[l].astype(BF16), row(ln2_g[l]), row(ln2_b[l]), alpha, seq)

        h = _ffn_call(h, mod, ffn2_w_in[l].astype(BF16), ffn2_w_out[l].astype(BF16), row(ln3_g[l]), row(ln3_b[l]),
                      (6, 7, 8), alpha, seq)
    return h.reshape(batch, seq, d)
```

```python
import functools

import jax
import jax.numpy as jnp
from jax import lax
from jax.experimental import pallas as pl
from jax.experimental.pallas import tpu as pltpu

F32 = jnp.float32
BF16 = jnp.bfloat16

CHUNK = 64
A_HEADS = 16
A_HEAD_DIM = 64
A_WIDTH = A_HEADS * A_HEAD_DIM
A_PAST_CHUNKS = 8
REL_CLIP = 256
B_HEADS = 4
GATE_RANK = 16
GATE_TAU = 16.0
N_MOD = 9
LN_EPS = 1e-5
RMS_EPS = 1e-6

LANES = 128
SUBLANES = 8
VMEM_LIMIT_BYTES = 56 * 1024 * 1024

FFN_OUT_CHUNKS = 4

Q_CHUNKS = 4
Q_SUB = Q_CHUNKS * CHUNK
K_WIN = (A_PAST_CHUNKS + Q_CHUNKS) * CHUNK
ATTN_UNROLL = 7
E_LEN = Q_SUB + K_WIN


def _params(*sem):
    return pltpu.CompilerParams(dimension_semantics=sem, vmem_limit_bytes=VMEM_LIMIT_BYTES)


def _silu(v):
    return v * jax.nn.sigmoid(v)


def _layer_norm(y, g, b):
    mu = jnp.mean(y, axis=-1, keepdims=True)
    d = y - mu
    var = jnp.mean(d * d, axis=-1, keepdims=True)
    return d * lax.rsqrt(var + LN_EPS) * g + b


def _mod_kernel(c_ref, w_ref, b_ref, o_ref):
    s = _silu(c_ref[...]).astype(BF16)
    o_ref[...] = jnp.dot(s, w_ref[...].astype(BF16), preferred_element_type=F32) + b_ref[...]


def _mod_call(c_pad, w_ada, b_ada, tn=1024):
    rows, d = c_pad.shape
    n = w_ada.shape[1]
    return pl.pallas_call(
        _mod_kernel,
        grid=(n // tn,),
        in_specs=[
            pl.BlockSpec((rows, d), lambda j: (0, 0)),
            pl.BlockSpec((d, tn), lambda j: (0, j)),
            pl.BlockSpec((1, tn), lambda j: (0, j)),
        ],
        out_specs=pl.BlockSpec((rows, tn), lambda j: (0, j)),
        out_shape=jax.ShapeDtypeStruct((rows, n), F32),
        compiler_params=_params("parallel"),
        name="mod",
    )(c_pad, w_ada, b_ada)


def _ffn_kernel(h_ref, mod_ref, wa_ref, wb_ref, wo_ref, g_ref, b_ref, o_ref, u_ref, *, rows, alpha, out_chunks):
    j = pl.program_id(1)
    last = pl.num_programs(1) - 1
    sh, sc, gt = rows

    def gated(u):
        a = jnp.dot(u, wa_ref[...], preferred_element_type=F32)
        b = jnp.dot(u, wb_ref[...], preferred_element_type=F32)
        return (_silu(a) * b).astype(BF16)

    @pl.when(j == 0)
    def _():
        u = (h_ref[...] * (1.0 + mod_ref[0, sc:sc + 1, :]) + mod_ref[0, sh:sh + 1, :]).astype(BF16)
        u_ref[...] = u
        o_ref[...] = jnp.dot(gated(u), wo_ref[...], preferred_element_type=F32)

    @pl.when(jnp.logical_and(j > 0, j < last))
    def _():
        o_ref[...] += jnp.dot(gated(u_ref[...]), wo_ref[...], preferred_element_type=F32)

    @pl.when(j == last)
    def _():
        g = gated(u_ref[...])
        tr = o_ref.shape[0] // out_chunks
        for r in range(out_chunks):
            rs = slice(r * tr, (r + 1) * tr)
            f = o_ref[rs, :] + jnp.dot(g[rs], wo_ref[...], preferred_element_type=F32)
            y = alpha * h_ref[rs, :] + (0.5 * mod_ref[0, gt:gt + 1, :]) * f
            o_ref[rs, :] = _layer_norm(y, g_ref[...], b_ref[...])


def _ffn_call(h, mod, w_in, w_out, ln_g, ln_b, rows, alpha, seq, tm=1024, tf=512):
    n, d = h.shape
    dff = w_out.shape[0]
    nj = dff // tf
    per_batch = seq // tm
    return pl.pallas_call(
        functools.partial(_ffn_kernel, rows=rows, alpha=alpha, out_chunks=FFN_OUT_CHUNKS),
        grid=(n // tm, nj),
        in_specs=[
            pl.BlockSpec((tm, d), lambda i, j: (i, 0)),
            pl.BlockSpec((1, N_MOD, d), lambda i, j: (i // per_batch, 0, 0)),
            pl.BlockSpec((d, tf), lambda i, j: (0, j)),
            pl.BlockSpec((d, tf), lambda i, j: (0, j + nj)),
            pl.BlockSpec((tf, d), lambda i, j: (j, 0)),
            pl.BlockSpec((1, d), lambda i, j: (0, 0)),
            pl.BlockSpec((1, d), lambda i, j: (0, 0)),
        ],
        out_specs=pl.BlockSpec((tm, d), lambda i, j: (i, 0)),
        out_shape=jax.ShapeDtypeStruct((n, d), F32),
        scratch_shapes=[pltpu.VMEM((tm, d), BF16)],
        compiler_params=_params("parallel", "arbitrary"),
        name="ffn",
    )(h, mod, w_in, w_in, w_out, ln_g, ln_b)


def _repack_kernel(w_ref, wlr_ref, main_ref, lr_ref):
    main_ref[...] = w_ref[...].astype(BF16)
    row = lax.broadcasted_iota(jnp.int32, lr_ref.shape, 0)
    lr_ref[...] = jnp.where(row < GATE_RANK, wlr_ref[...], 0.0).astype(BF16)


def _repack_call(w_t, n_main, tr=1024):
    rows, d = w_t.shape
    main_blocks = n_main // tr
    src = lambda k: (SUBLANES * (k * (tr // SUBLANES) + jnp.where(k < main_blocks, 0, GATE_RANK // SUBLANES)), 0)
    return pl.pallas_call(
        _repack_kernel,
        grid=((rows - GATE_RANK) // tr,),
        in_specs=[
            pl.BlockSpec((pl.Element(tr), pl.Element(d)), src),
            pl.BlockSpec((pl.Element(LANES), pl.Element(d)), lambda k: (n_main, 0)),
        ],
        out_specs=[
            pl.BlockSpec((tr, d), lambda k: (k, 0)),
            pl.BlockSpec((LANES, d), lambda k: (0, 0)),
        ],
        out_shape=[
            jax.ShapeDtypeStruct((rows - GATE_RANK, d), BF16),
            jax.ShapeDtypeStruct((LANES, d), BF16),
        ],
        compiler_params=_params("arbitrary"),
        name="repack",
    )(w_t, w_t)


_NT = (((1,), (1,)), ((), ()))


def _mix_in_kernel(h_ref, mod_ref, w_ref, wlr_ref, p_ref, lr_ref, u_ref):
    j = pl.program_id(1)

    @pl.when(j == 0)
    def _():
        u = (h_ref[...] * (1.0 + mod_ref[0, 4:5, :]) + mod_ref[0, 3:4, :]).astype(BF16)
        u_ref[...] = u
        p_ref[...] = lax.dot_general(u, w_ref[...], _NT, preferred_element_type=F32).astype(BF16)
        lr_ref[...] = lax.dot_general(u, wlr_ref[...], _NT, preferred_element_type=F32)

    @pl.when(j > 0)
    def _():
        p_ref[...] = lax.dot_general(u_ref[...], w_ref[...], _NT, preferred_element_type=F32).astype(BF16)


def _mix_in_call(h, mod, w_main_t, w_lr_t, seq, tm=1024, tn=2048):
    n, d = h.shape
    cols = w_main_t.shape[0]
    per_batch = seq // tm
    return pl.pallas_call(
        _mix_in_kernel,
        grid=(n // tm, cols // tn),
        in_specs=[
            pl.BlockSpec((tm, d), lambda i, j: (i, 0)),
            pl.BlockSpec((1, N_MOD, d), lambda i, j: (i // per_batch, 0, 0)),
            pl.BlockSpec((tn, d), lambda i, j: (j, 0)),
            pl.BlockSpec((LANES, d), lambda i, j: (0, 0)),
        ],
        out_specs=[
            pl.BlockSpec((tm, tn), lambda i, j: (i, j)),
            pl.BlockSpec((tm, LANES), lambda i, j: (i, 0)),
        ],
        out_shape=[
            jax.ShapeDtypeStruct((n, cols), BF16),
            jax.ShapeDtypeStruct((n, LANES), F32),
        ],
        scratch_shapes=[pltpu.VMEM((tm, d), BF16)],
        compiler_params=_params("parallel", "arbitrary"),
        name="mix_in",
    )(h, mod, w_main_t, w_lr_t)


def _bias_table_kernel(e_ref, t_ref):
    row = lax.broadcasted_iota(jnp.int32, (Q_SUB, K_WIN), 0)
    col = lax.broadcasted_iota(jnp.int32, (Q_SUB, K_WIN), 1)
    q_chunk = row // CHUNK
    k_chunk = col // CHUNK
    in_band = (k_chunk >= q_chunk) & (k_chunk <= q_chunk + A_PAST_CHUNKS)
    for h in range(2):
        e = jnp.broadcast_to(e_ref[h], (Q_SUB, E_LEN))
        t = pltpu.roll(e, E_LEN - (Q_SUB - 1), 1, stride=1, stride_axis=0)[:, :K_WIN]
        t_ref[0, h * Q_SUB:(h + 1) * Q_SUB, :] = jnp.where(in_band, t, -jnp.inf)


def _bias_table_call(e_ext):
    pairs = e_ext.shape[0] // 2
    return pl.pallas_call(
        _bias_table_kernel,
        grid=(pairs,),
        in_specs=[pl.BlockSpec((2, 1, E_LEN), lambda hp: (hp, 0, 0))],
        out_specs=pl.BlockSpec((1, 2 * Q_SUB, K_WIN), lambda hp: (hp, 0, 0)),
        out_shape=jax.ShapeDtypeStruct((pairs, 2 * Q_SUB, K_WIN), F32),
        compiler_params=_params("parallel"),
        name="bias_table",
    )(e_ext)


def _attn_kernel(q_ref, k_ref, v_ref, t_ref, *refs, seq):
    n_cast = len(refs) // 2
    o_ref = refs[n_cast]
    for src_ref, dst_ref in zip(refs[:n_cast], refs[n_cast + 1:]):
        dst_ref[...] = src_ref[...].astype(dst_ref.dtype)

    lane = lax.broadcasted_iota(jnp.int32, (Q_SUB, 2 * A_HEAD_DIM), 1)
    head0 = lane < A_HEAD_DIM

    def sub_tile(q_start, k_start, width, t_start):
        q = q_ref[pl.ds(q_start, Q_SUB), :].astype(F32) * (A_HEAD_DIM ** -0.5)
        q2 = jnp.concatenate([jnp.where(head0, q, 0.0), jnp.where(head0, 0.0, q)], axis=0).astype(BF16)
        kw = k_ref[pl.ds(k_start, width), :]
        vw = v_ref[pl.ds(k_start, width), :]
        s = lax.dot_general(q2, kw, (((1,), (1,)), ((), ())), preferred_element_type=F32)
        s = s + t_ref[0, :, t_start:t_start + width]
        m = jnp.max(s, axis=-1, keepdims=True)
        p = jnp.exp(s - m)
        l = jnp.sum(p, axis=-1, keepdims=True)
        o = jnp.dot(p.astype(BF16), vw, preferred_element_type=F32) / l
        o_ref[pl.ds(q_start, Q_SUB), :] = jnp.where(head0, o[:Q_SUB], o[Q_SUB:]).astype(o_ref.dtype)

    n_head = (K_WIN - Q_SUB) // Q_SUB
    for j in range(n_head):
        sub_tile(j * Q_SUB, 0, (j + 1) * Q_SUB, K_WIN - (j + 1) * Q_SUB)

    def body(j, carry):
        q_start = pl.multiple_of(j * Q_SUB, Q_SUB)
        k_start = pl.multiple_of(j * Q_SUB - (K_WIN - Q_SUB), Q_SUB)
        sub_tile(q_start, k_start, K_WIN, 0)
        return carry

    lax.fori_loop(n_head, seq // Q_SUB, body, 0, unroll=ATTN_UNROLL)


def _attn_call(p, table, batch, seq, cast_weights):
    n = p.shape[0]
    pairs = A_HEADS // 2
    w = 2 * A_HEAD_DIM
    steps = batch * pairs
    slab = lambda a: pl.BlockSpec((a.shape[0] // steps, a.shape[1]), lambda b, hp: (b * pairs + hp, 0))
    outs = pl.pallas_call(
        functools.partial(_attn_kernel, seq=seq),
        grid=(batch, pairs),
        in_specs=[
            pl.BlockSpec((seq, w), lambda b, hp: (b, hp)),
            pl.BlockSpec((seq, w), lambda b, hp: (b, pairs + hp)),
            pl.BlockSpec((seq, w), lambda b, hp: (b, 2 * pairs + hp)),
            pl.BlockSpec((1, 2 * Q_SUB, K_WIN), lambda b, hp: (hp, 0, 0)),
        ] + [slab(a) for a in cast_weights],
        out_specs=[pl.BlockSpec((seq, w), lambda b, hp: (b, hp))] + [slab(a) for a in cast_weights],
        out_shape=[jax.ShapeDtypeStruct((n, A_WIDTH), BF16)]
        + [jax.ShapeDtypeStruct(a.shape, BF16) for a in cast_weights],
        compiler_params=_params("parallel", "parallel"),
        name="attn",
    )(p, p, p, table, *cast_weights)
    return outs[0], outs[1:]


def _gla_kernel(q_ref, k_ref, v_ref, r_ref, lr_ref, wa_ref, ba_ref, gn_ref, o_ref, st_ref, *, chunks, hk, hv):
    @pl.when(pl.program_id(1) == 0)
    def _():
        st_ref[...] = jnp.zeros_like(st_ref)

    ri = lax.broadcasted_iota(jnp.int32, (CHUNK, CHUNK), 0)
    ci = lax.broadcasted_iota(jnp.int32, (CHUNK, CHUNK), 1)
    tri = (ci <= ri).astype(BF16)

    state = [st_ref[h] for h in range(B_HEADS)]
    for n in range(chunks):
        r0 = n * CHUNK
        z = jnp.dot(lr_ref[r0:r0 + CHUNK, :].astype(BF16), wa_ref[...], preferred_element_type=F32) + ba_ref[...]
        log_a = -(jnp.maximum(-z, 0.0) + jnp.log(1.0 + jnp.exp(-jnp.abs(z)))) / GATE_TAU
        x0 = log_a.astype(BF16)
        x1 = (log_a - x0.astype(F32)).astype(BF16)
        cum = jnp.dot(tri, x0, preferred_element_type=F32) + jnp.dot(tri, x1, preferred_element_type=F32)
        last = cum[CHUNK - 1:CHUNK, :]
        kdec = (k_ref[r0:r0 + CHUNK, :].astype(F32) * jnp.exp(last - cum)).astype(BF16)
        decay = jnp.exp(last)
        for h in range(B_HEADS):
            ks = slice(h * hk, (h + 1) * hk)
            vs = slice(h * hv, (h + 1) * hv)
            upd = lax.dot_general(v_ref[r0:r0 + CHUNK, vs], kdec[:, ks], (((0,), (0,)), ((), ())),
                                  preferred_element_type=F32)
            st = state[h] * decay[:, ks] + upd
            state[h] = st
            o = lax.dot_general(q_ref[r0:r0 + CHUNK, ks], st.astype(BF16), (((1,), (1,)), ((), ())),
                                preferred_element_type=F32)
            o = o * lax.rsqrt(jnp.mean(o * o, axis=-1, keepdims=True) + RMS_EPS * hk) * gn_ref[...]
            o_ref[r0:r0 + CHUNK, vs] = (o * _silu(r_ref[r0:r0 + CHUNK, vs].astype(F32))).astype(o_ref.dtype)

    for h in range(B_HEADS):
        st_ref[h] = state[h]


def _gla_call(p, lr, w_alpha2, b_alpha, gn, batch, seq, kdim, vdim, rows=512):
    n = p.shape[0]
    steps = seq // rows
    hk, hv = kdim // B_HEADS, vdim // B_HEADS
    q_blk = (3 * A_WIDTH) // kdim
    v_blk = (3 * A_WIDTH + 2 * kdim) // vdim
    row_map = lambda col: (lambda b, t: (b * steps + t, col))
    return pl.pallas_call(
        functools.partial(_gla_kernel, chunks=rows // CHUNK, hk=hk, hv=hv),
        grid=(batch, steps),
        in_specs=[
            pl.BlockSpec((rows, kdim), row_map(q_blk)),
            pl.BlockSpec((rows, kdim), row_map(q_blk + 1)),
            pl.BlockSpec((rows, vdim), row_map(v_blk)),
            pl.BlockSpec((rows, vdim), row_map(v_blk + 1)),
            pl.BlockSpec((rows, LANES), row_map(0)),
            pl.BlockSpec((LANES, kdim), lambda b, t: (0, 0)),
            pl.BlockSpec((1, kdim), lambda b, t: (0, 0)),
            pl.BlockSpec((1, hv), lambda b, t: (0, 0)),
        ],
        out_specs=pl.BlockSpec((rows, vdim), row_map(0)),
        out_shape=jax.ShapeDtypeStruct((n, vdim), BF16),
        scratch_shapes=[pltpu.VMEM((B_HEADS, hv, hk), F32)],
        compiler_params=_params("parallel", "arbitrary"),
        name="gla",
    )(p, p, p, p, lr, w_alpha2, b_alpha, gn)


def _merge_kernel(ya_ref, yb_ref, ga_ref, gb_ref, h_ref, mod_ref, wpa_ref, wpb_ref, wo_ref, g_ref, b_ref, o_ref,
                  *, alpha, chunks):
    tr = o_ref.shape[0] // chunks
    for r in range(chunks):
        rs = slice(r * tr, (r + 1) * tr)
        pa = jnp.dot(ya_ref[rs, :], wpa_ref[...], preferred_element_type=F32)
        pb = jnp.dot(yb_ref[rs, :], wpb_ref[...], preferred_element_type=F32)
        merged = (jax.nn.sigmoid(ga_ref[rs, :].astype(F32)) * pa
                  + jax.nn.sigmoid(gb_ref[rs, :].astype(F32)) * pb)
        m = jnp.dot(merged.astype(BF16), wo_ref[...], preferred_element_type=F32)
        y = alpha * h_ref[rs, :] + mod_ref[0, 5:6, :] * m
        o_ref[rs, :] = _layer_norm(y, g_ref[...], b_ref[...])


def _merge_call(ya, yb, p, h, mod, w_pa, w_pb, w_out, ln_g, ln_b, alpha, seq, tm=512, chunks=2):
    n, d = h.shape
    per_batch = seq // tm
    ga_blk = (p.shape[1] - 2 * d) // d
    const = lambda shape: pl.BlockSpec(shape, lambda i: (0, 0), pipeline_mode=pl.Buffered(1))
    return pl.pallas_call(
        functools.partial(_merge_kernel, alpha=alpha, chunks=chunks),
        grid=(n // tm,),
        in_specs=[
            pl.BlockSpec((tm, ya.shape[1]), lambda i: (i, 0)),
            pl.BlockSpec((tm, yb.shape[1]), lambda i: (i, 0)),
            pl.BlockSpec((tm, d), lambda i: (i, ga_blk)),
            pl.BlockSpec((tm, d), lambda i: (i, ga_blk + 1)),
            pl.BlockSpec((tm, d), lambda i: (i, 0)),
            pl.BlockSpec((1, N_MOD, d), lambda i: (i // per_batch, 0, 0)),
            const(w_pa.shape),
            const(w_pb.shape),
            const(w_out.shape),
            const((1, d)),
            const((1, d)),
        ],
        out_specs=pl.BlockSpec((tm, d), lambda i: (i, 0)),
        out_shape=jax.ShapeDtypeStruct((n, d), F32),
        compiler_params=_params("parallel"),
        name="merge",
    )(ya, yb, p, p, h, mod, w_pa, w_pb, w_out, ln_g, ln_b)


def _extended_rel_bias(rel_bias):
    heads, rel_size = rel_bias.shape
    n_low = (Q_SUB - 1) + (K_WIN - Q_SUB) - REL_CLIP
    n_high = E_LEN - n_low - rel_size
    e = jnp.concatenate([
        jnp.broadcast_to(rel_bias[:, :1], (heads, n_low)),
        rel_bias,
        jnp.broadcast_to(rel_bias[:, -1:], (heads, n_high)),
    ], axis=1)
    return e.reshape(heads, 1, E_LEN).astype(F32)


def kernel(x, c, w_ada, b_ada, ffn1_w_in, ffn1_w_out, ln1_g, ln1_b, w_mix_in, rel_bias, w_alpha2, b_alpha,
           gla_norm_g, w_proj_a, w_proj_b, w_mix_out, ln2_g, ln2_b, ffn2_w_in, ffn2_w_out, ln3_g, ln3_b):
    batch, seq, d = x.shape
    depth = w_ada.shape[0]
    alpha = (2.0 * depth) ** 0.25
    kdim = w_alpha2.shape[2]
    vdim = w_proj_b.shape[1]
    n_main = 3 * A_WIDTH + 2 * kdim + 2 * vdim
    row = lambda v: v.reshape(1, -1)

    h = x.reshape(batch * seq, d)
    c_pad = jnp.zeros((16, d), F32).at[:batch].set(c)
    for l in range(depth):
        mod = _mod_call(c_pad, w_ada[l], row(b_ada[l]))[:batch].reshape(batch, N_MOD, d)

        h = _ffn_call(h, mod, ffn1_w_in[l].astype(BF16), ffn1_w_out[l].astype(BF16), row(ln1_g[l]), row(ln1_b[l]),
                      (0, 1, 2), alpha, seq)

        w_main_t, w_lr_t = _repack_call(jnp.swapaxes(w_mix_in, 1, 2)[l], n_main)
        p, lr = _mix_in_call(h, mod, w_main_t, w_lr_t, seq)

        ya, (w_pa, w_pb, w_mo, w2_in, w2_out) = _attn_call(
            p, _bias_table_call(_extended_rel_bias(rel_bias[l])), batch, seq,
            (w_proj_a[l], w_proj_b[l], w_mix_out[l], ffn2_w_in[l], ffn2_w_out[l]))
        w_a2 = jnp.zeros((LANES, kdim), BF16).at[:GATE_RANK].set(w_alpha2[l].astype(BF16))
        yb = _gla_call(p, lr, w_a2, row(b_alpha[l]), row(gla_norm_g[l]), batch, seq, kdim, vdim)

        h = _merge_call(ya, yb, p, h, mod, w_pa, w_pb, w_mo, row(ln2_g[l]), row(ln2_b[l]), alpha, seq)

        h = _ffn_call(h, mod, w2_in, w2_out, row(ln3_g[l]), row(ln3_b[l]), (6, 7, 8), alpha, seq)
    return h.reshape(batch, seq, d)
```

```python
import functools

import jax
import jax.numpy as jnp
from jax import lax
from jax.experimental import pallas as pl
from jax.experimental.pallas import tpu as pltpu

F32 = jnp.float32
BF16 = jnp.bfloat16

CHUNK = 64
A_HEADS = 16
A_HEAD_DIM = 64
A_WIDTH = A_HEADS * A_HEAD_DIM
A_PAST_CHUNKS = 8
REL_CLIP = 256
B_HEADS = 4
GATE_RANK = 16
GATE_TAU = 16.0
N_MOD = 9
LN_EPS = 1e-5
RMS_EPS = 1e-6

LANES = 128
SUBLANES = 8
VMEM_LIMIT_BYTES = 56 * 1024 * 1024

FFN_OUT_CHUNKS = 4

Q_CHUNKS = 4
Q_SUB = Q_CHUNKS * CHUNK
K_WIN = (A_PAST_CHUNKS + Q_CHUNKS) * CHUNK
ATTN_UNROLL = 14
E_LEN = Q_SUB + K_WIN


def _params(*sem):
    return pltpu.CompilerParams(dimension_semantics=sem, vmem_limit_bytes=VMEM_LIMIT_BYTES)


def _silu(v):
    return v * jax.nn.sigmoid(v)


def _layer_norm(y, g, b):
    mu = jnp.mean(y, axis=-1, keepdims=True)
    d = y - mu
    var = jnp.mean(d * d, axis=-1, keepdims=True)
    return d * lax.rsqrt(var + LN_EPS) * g + b


def _mod_kernel(c_ref, w_ref, b_ref, o_ref):
    s = _silu(c_ref[...]).astype(BF16)
    o_ref[...] = jnp.dot(s, w_ref[...].astype(BF16), preferred_element_type=F32) + b_ref[...]


def _mod_call(c_pad, w_ada, b_ada, tn=1024):
    rows, d = c_pad.shape
    n = w_ada.shape[1]
    return pl.pallas_call(
        _mod_kernel,
        grid=(n // tn,),
        in_specs=[
            pl.BlockSpec((rows, d), lambda j: (0, 0)),
            pl.BlockSpec((d, tn), lambda j: (0, j)),
            pl.BlockSpec((1, tn), lambda j: (0, j)),
        ],
        out_specs=pl.BlockSpec((rows, tn), lambda j: (0, j)),
        out_shape=jax.ShapeDtypeStruct((rows, n), F32),
        compiler_params=_params("parallel"),
        name="mod",
    )(c_pad, w_ada, b_ada)


def _ffn_kernel(h_ref, mod_ref, wa_ref, wb_ref, wo_ref, g_ref, b_ref, o_ref, u_ref, *, rows, alpha, out_chunks):
    j = pl.program_id(1)
    last = pl.num_programs(1) - 1
    sh, sc, gt = rows

    def gated(u):
        a = jnp.dot(u, wa_ref[...], preferred_element_type=F32)
        b = jnp.dot(u, wb_ref[...], preferred_element_type=F32)
        return (_silu(a) * b).astype(BF16)

    @pl.when(j == 0)
    def _():
        u = (h_ref[...] * (1.0 + mod_ref[0, sc:sc + 1, :]) + mod_ref[0, sh:sh + 1, :]).astype(BF16)
        u_ref[...] = u
        o_ref[...] = jnp.dot(gated(u), wo_ref[...], preferred_element_type=F32)

    @pl.when(jnp.logical_and(j > 0, j < last))
    def _():
        o_ref[...] += jnp.dot(gated(u_ref[...]), wo_ref[...], preferred_element_type=F32)

    @pl.when(j == last)
    def _():
        g = gated(u_ref[...])
        tr = o_ref.shape[0] // out_chunks
        for r in range(out_chunks):
            rs = slice(r * tr, (r + 1) * tr)
            f = o_ref[rs, :] + jnp.dot(g[rs], wo_ref[...], preferred_element_type=F32)
            y = alpha * h_ref[rs, :] + (0.5 * mod_ref[0, gt:gt + 1, :]) * f
            o_ref[rs, :] = _layer_norm(y, g_ref[...], b_ref[...])


def _ffn_call(h, mod, w_in, w_out, ln_g, ln_b, rows, alpha, seq, tm=1024, tf=512):
    n, d = h.shape
    dff = w_out.shape[0]
    nj = dff // tf
    per_batch = seq // tm
    return pl.pallas_call(
        functools.partial(_ffn_kernel, rows=rows, alpha=alpha, out_chunks=FFN_OUT_CHUNKS),
        grid=(n // tm, nj),
        in_specs=[
            pl.BlockSpec((tm, d), lambda i, j: (i, 0)),
            pl.BlockSpec((1, N_MOD, d), lambda i, j: (i // per_batch, 0, 0)),
            pl.BlockSpec((d, tf), lambda i, j: (0, j)),
            pl.BlockSpec((d, tf), lambda i, j: (0, j + nj)),
            pl.BlockSpec((tf, d), lambda i, j: (j, 0)),
            pl.BlockSpec((1, d), lambda i, j: (0, 0)),
            pl.BlockSpec((1, d), lambda i, j: (0, 0)),
        ],
        out_specs=pl.BlockSpec((tm, d), lambda i, j: (i, 0)),
        out_shape=jax.ShapeDtypeStruct((n, d), F32),
        scratch_shapes=[pltpu.VMEM((tm, d), BF16)],
        compiler_params=_params("parallel", "arbitrary"),
        name="ffn",
    )(h, mod, w_in, w_in, w_out, ln_g, ln_b)


def _repack_kernel(w_ref, wlr_ref, main_ref, lr_ref):
    main_ref[...] = w_ref[...].astype(BF16)
    row = lax.broadcasted_iota(jnp.int32, lr_ref.shape, 0)
    lr_ref[...] = jnp.where(row < GATE_RANK, wlr_ref[...], 0.0).astype(BF16)


def _repack_call(w_t, n_main, tr=1024):
    rows, d = w_t.shape
    main_blocks = n_main // tr
    src = lambda k: (SUBLANES * (k * (tr // SUBLANES) + jnp.where(k < main_blocks, 0, GATE_RANK // SUBLANES)), 0)
    return pl.pallas_call(
        _repack_kernel,
        grid=((rows - GATE_RANK) // tr,),
        in_specs=[
            pl.BlockSpec((pl.Element(tr), pl.Element(d)), src),
            pl.BlockSpec((pl.Element(LANES), pl.Element(d)), lambda k: (n_main, 0)),
        ],
        out_specs=[
            pl.BlockSpec((tr, d), lambda k: (k, 0)),
            pl.BlockSpec((LANES, d), lambda k: (0, 0)),
        ],
        out_shape=[
            jax.ShapeDtypeStruct((rows - GATE_RANK, d), BF16),
            jax.ShapeDtypeStruct((LANES, d), BF16),
        ],
        compiler_params=_params("arbitrary"),
        name="repack",
    )(w_t, w_t)


_NT = (((1,), (1,)), ((), ()))


def _mix_in_kernel(h_ref, mod_ref, w_ref, wlr_ref, p_ref, lr_ref, u_ref):
    j = pl.program_id(1)

    @pl.when(j == 0)
    def _():
        u = (h_ref[...] * (1.0 + mod_ref[0, 4:5, :]) + mod_ref[0, 3:4, :]).astype(BF16)
        u_ref[...] = u
        p_ref[...] = lax.dot_general(u, w_ref[...], _NT, preferred_element_type=F32).astype(BF16)
        lr_ref[...] = lax.dot_general(u, wlr_ref[...], _NT, preferred_element_type=F32)

    @pl.when(j > 0)
    def _():
        p_ref[...] = lax.dot_general(u_ref[...], w_ref[...], _NT, preferred_element_type=F32).astype(BF16)


def _mix_in_call(h, mod, w_main_t, w_lr_t, seq, tm=1024, tn=2048):
    n, d = h.shape
    cols = w_main_t.shape[0]
    per_batch = seq // tm
    return pl.pallas_call(
        _mix_in_kernel,
        grid=(n // tm, cols // tn),
        in_specs=[
            pl.BlockSpec((tm, d), lambda i, j: (i, 0)),
            pl.BlockSpec((1, N_MOD, d), lambda i, j: (i // per_batch, 0, 0)),
            pl.BlockSpec((tn, d), lambda i, j: (j, 0)),
            pl.BlockSpec((LANES, d), lambda i, j: (0, 0)),
        ],
        out_specs=[
            pl.BlockSpec((tm, tn), lambda i, j: (i, j)),
            pl.BlockSpec((tm, LANES), lambda i, j: (i, 0)),
        ],
        out_shape=[
            jax.ShapeDtypeStruct((n, cols), BF16),
            jax.ShapeDtypeStruct((n, LANES), F32),
        ],
        scratch_shapes=[pltpu.VMEM((tm, d), BF16)],
        compiler_params=_params("parallel", "arbitrary"),
        name="mix_in",
    )(h, mod, w_main_t, w_lr_t)


def _bias_table_kernel(e_ref, t_ref):
    row = lax.broadcasted_iota(jnp.int32, (Q_SUB, K_WIN), 0)
    col = lax.broadcasted_iota(jnp.int32, (Q_SUB, K_WIN), 1)
    q_chunk = row // CHUNK
    k_chunk = col // CHUNK
    in_band = (k_chunk >= q_chunk) & (k_chunk <= q_chunk + A_PAST_CHUNKS)
    for h in range(2):
        e = jnp.broadcast_to(e_ref[h], (Q_SUB, E_LEN))
        t = pltpu.roll(e, E_LEN - (Q_SUB - 1), 1, stride=1, stride_axis=0)[:, :K_WIN]
        t_ref[0, h * Q_SUB:(h + 1) * Q_SUB, :] = jnp.where(in_band, t, -jnp.inf)


def _bias_table_call(e_ext):
    pairs = e_ext.shape[0] // 2
    return pl.pallas_call(
        _bias_table_kernel,
        grid=(pairs,),
        in_specs=[pl.BlockSpec((2, 1, E_LEN), lambda hp: (hp, 0, 0))],
        out_specs=pl.BlockSpec((1, 2 * Q_SUB, K_WIN), lambda hp: (hp, 0, 0)),
        out_shape=jax.ShapeDtypeStruct((pairs, 2 * Q_SUB, K_WIN), F32),
        compiler_params=_params("parallel"),
        name="bias_table",
    )(e_ext)


def _attn_kernel(q_ref, k_ref, v_ref, t_ref, *refs, seq):
    n_cast = len(refs) // 2
    o_ref = refs[n_cast]
    for src_ref, dst_ref in zip(refs[:n_cast], refs[n_cast + 1:]):
        dst_ref[...] = src_ref[...].astype(dst_ref.dtype)

    lane = lax.broadcasted_iota(jnp.int32, (Q_SUB, 2 * A_HEAD_DIM), 1)
    head0 = lane < A_HEAD_DIM

    def sub_tile(q_start, k_start, width, t_start):
        q = q_ref[pl.ds(q_start, Q_SUB), :].astype(F32) * (A_HEAD_DIM ** -0.5)
        q2 = jnp.concatenate([jnp.where(head0, q, 0.0), jnp.where(head0, 0.0, q)], axis=0).astype(BF16)
        kw = k_ref[pl.ds(k_start, width), :]
        vw = v_ref[pl.ds(k_start, width), :]
        s = lax.dot_general(q2, kw, (((1,), (1,)), ((), ())), preferred_element_type=F32)
        s = s + t_ref[0, :, t_start:t_start + width]
        m = jnp.max(s, axis=-1, keepdims=True)
        p = jnp.exp(s - m)
        l = jnp.sum(p, axis=-1, keepdims=True)
        o = jnp.dot(p.astype(BF16), vw, preferred_element_type=F32) / l
        o_ref[pl.ds(q_start, Q_SUB), :] = jnp.where(head0, o[:Q_SUB], o[Q_SUB:]).astype(o_ref.dtype)

    n_head = (K_WIN - Q_SUB) // Q_SUB
    for j in range(n_head):
        sub_tile(j * Q_SUB, 0, (j + 1) * Q_SUB, K_WIN - (j + 1) * Q_SUB)

    def body(j, carry):
        q_start = pl.multiple_of(j * Q_SUB, Q_SUB)
        k_start = pl.multiple_of(j * Q_SUB - (K_WIN - Q_SUB), Q_SUB)
        sub_tile(q_start, k_start, K_WIN, 0)
        return carry

    lax.fori_loop(n_head, seq // Q_SUB, body, 0, unroll=ATTN_UNROLL)


def _attn_call(p, table, batch, seq, cast_weights):
    n = p.shape[0]
    pairs = A_HEADS // 2
    w = 2 * A_HEAD_DIM
    steps = batch * pairs
    slab = lambda a: pl.BlockSpec((a.shape[0] // steps, a.shape[1]), lambda b, hp: (b * pairs + hp, 0))
    outs = pl.pallas_call(
        functools.partial(_attn_kernel, seq=seq),
        grid=(batch, pairs),
        in_specs=[
            pl.BlockSpec((seq, w), lambda b, hp: (b, hp)),
            pl.BlockSpec((seq, w), lambda b, hp: (b, pairs + hp)),
            pl.BlockSpec((seq, w), lambda b, hp: (b, 2 * pairs + hp)),
            pl.BlockSpec((1, 2 * Q_SUB, K_WIN), lambda b, hp: (hp, 0, 0)),
        ] + [slab(a) for a in cast_weights],
        out_specs=[pl.BlockSpec((seq, w), lambda b, hp: (b, hp))] + [slab(a) for a in cast_weights],
        out_shape=[jax.ShapeDtypeStruct((n, A_WIDTH), BF16)]
        + [jax.ShapeDtypeStruct(a.shape, BF16) for a in cast_weights],
        compiler_params=_params("parallel", "parallel"),
        name="attn",
    )(p, p, p, table, *cast_weights)
    return outs[0], outs[1:]


def _gla_kernel(q_ref, k_ref, v_ref, r_ref, lr_ref, wa_ref, ba_ref, gn_ref, o_ref, st_ref, readout_ref,
                *, chunks, hk, hv):
    @pl.when(pl.program_id(1) == 0)
    def _():
        st_ref[...] = jnp.zeros_like(st_ref)

    rows = chunks * CHUNK
    z = jnp.dot(lr_ref[...].astype(BF16), wa_ref[...], preferred_element_type=F32) + ba_ref[...]
    log_a = -(jnp.maximum(-z, 0.0) + jnp.log(1.0 + jnp.exp(-jnp.abs(z)))) / GATE_TAU
    x0 = log_a.astype(BF16)
    x1 = (log_a - x0.astype(F32)).astype(BF16)
    ri = lax.broadcasted_iota(jnp.int32, (rows, rows), 0)
    ci = lax.broadcasted_iota(jnp.int32, (rows, rows), 1)
    later = jnp.logical_and(ri // CHUNK == ci // CHUNK, ci > ri).astype(BF16)
    rest = jnp.dot(later, x0, preferred_element_type=F32) + jnp.dot(later, x1, preferred_element_type=F32)
    rn = lax.broadcasted_iota(jnp.int32, (chunks, rows), 0)
    cn = lax.broadcasted_iota(jnp.int32, (chunks, rows), 1)
    member = (cn // CHUNK == rn).astype(BF16)
    total = jnp.dot(member, x0, preferred_element_type=F32) + jnp.dot(member, x1, preferred_element_type=F32)
    decay = jnp.exp(total)
    kdec = (k_ref[...].astype(F32) * jnp.exp(rest)).astype(BF16)

    for n in range(chunks):
        rs = slice(n * CHUNK, (n + 1) * CHUNK)
        for h in range(B_HEADS):
            ks = slice(h * hk, (h + 1) * hk)
            vs = slice(h * hv, (h + 1) * hv)
            upd = lax.dot_general(v_ref[rs, vs], kdec[rs, ks], (((0,), (0,)), ((), ())),
                                  preferred_element_type=F32)
            st = st_ref[h] * decay[n:n + 1, ks] + upd
            st_ref[h] = st
            readout_ref[rs, vs] = lax.dot_general(q_ref[rs, ks], st.astype(BF16), _NT, preferred_element_type=F32)

    for h in range(B_HEADS):
        vs = slice(h * hv, (h + 1) * hv)
        o = readout_ref[:, vs]
        o = o * lax.rsqrt(jnp.mean(o * o, axis=-1, keepdims=True) + RMS_EPS * hk) * gn_ref[...]
        o_ref[:, vs] = (o * _silu(r_ref[:, vs].astype(F32))).astype(o_ref.dtype)


def _gla_call(p, lr, w_alpha2, b_alpha, gn, batch, seq, kdim, vdim, rows=512):
    n = p.shape[0]
    steps = seq // rows
    hk, hv = kdim // B_HEADS, vdim // B_HEADS
    q_blk = (3 * A_WIDTH) // kdim
    v_blk = (3 * A_WIDTH + 2 * kdim) // vdim
    row_map = lambda col: (lambda b, t: (b * steps + t, col))
    return pl.pallas_call(
        functools.partial(_gla_kernel, chunks=rows // CHUNK, hk=hk, hv=hv),
        grid=(batch, steps),
        in_specs=[
            pl.BlockSpec((rows, kdim), row_map(q_blk)),
            pl.BlockSpec((rows, kdim), row_map(q_blk + 1)),
            pl.BlockSpec((rows, vdim), row_map(v_blk)),
            pl.BlockSpec((rows, vdim), row_map(v_blk + 1)),
            pl.BlockSpec((rows, LANES), row_map(0)),
            pl.BlockSpec((LANES, kdim), lambda b, t: (0, 0)),
            pl.BlockSpec((1, kdim), lambda b, t: (0, 0)),
            pl.BlockSpec((1, hv), lambda b, t: (0, 0)),
        ],
        out_specs=pl.BlockSpec((rows, vdim), row_map(0)),
        out_shape=jax.ShapeDtypeStruct((n, vdim), BF16),
        scratch_shapes=[pltpu.VMEM((B_HEADS, hv, hk), F32), pltpu.VMEM((rows, vdim), F32)],
        compiler_params=_params("parallel", "arbitrary"),
        name="gla",
    )(p, p, p, p, lr, w_alpha2, b_alpha, gn)


def _merge_kernel(ya_ref, yb_ref, ga_ref, gb_ref, h_ref, mod_ref, wpa_ref, wpb_ref, wo_ref, g_ref, b_ref, o_ref,
                  *, alpha, chunks):
    tr = o_ref.shape[0] // chunks
    for r in range(chunks):
        rs = slice(r * tr, (r + 1) * tr)
        pa = jnp.dot(ya_ref[rs, :], wpa_ref[...], preferred_element_type=F32)
        pb = jnp.dot(yb_ref[rs, :], wpb_ref[...], preferred_element_type=F32)
        merged = (jax.nn.sigmoid(ga_ref[rs, :].astype(F32)) * pa
                  + jax.nn.sigmoid(gb_ref[rs, :].astype(F32)) * pb)
        m = jnp.dot(merged.astype(BF16), wo_ref[...], preferred_element_type=F32)
        y = alpha * h_ref[rs, :] + mod_ref[0, 5:6, :] * m
        o_ref[rs, :] = _layer_norm(y, g_ref[...], b_ref[...])


def _merge_call(ya, yb, p, h, mod, w_pa, w_pb, w_out, ln_g, ln_b, alpha, seq, tm=512, chunks=2):
    n, d = h.shape
    per_batch = seq // tm
    ga_blk = (p.shape[1] - 2 * d) // d
    const = lambda shape: pl.BlockSpec(shape, lambda i: (0, 0), pipeline_mode=pl.Buffered(1))
    return pl.pallas_call(
        functools.partial(_merge_kernel, alpha=alpha, chunks=chunks),
        grid=(n // tm,),
        in_specs=[
            pl.BlockSpec((tm, ya.shape[1]), lambda i: (i, 0)),
            pl.BlockSpec((tm, yb.shape[1]), lambda i: (i, 0)),
            pl.BlockSpec((tm, d), lambda i: (i, ga_blk)),
            pl.BlockSpec((tm, d), lambda i: (i, ga_blk + 1)),
            pl.BlockSpec((tm, d), lambda i: (i, 0)),
            pl.BlockSpec((1, N_MOD, d), lambda i: (i // per_batch, 0, 0)),
            const(w_pa.shape),
            const(w_pb.shape),
            const(w_out.shape),
            const((1, d)),
            const((1, d)),
        ],
        out_specs=pl.BlockSpec((tm, d), lambda i: (i, 0)),
        out_shape=jax.ShapeDtypeStruct((n, d), F32),
        compiler_params=_params("parallel"),
        name="merge",
    )(ya, yb, p, p, h, mod, w_pa, w_pb, w_out, ln_g, ln_b)


def _extended_rel_bias(rel_bias):
    heads, rel_size = rel_bias.shape
    n_low = (Q_SUB - 1) + (K_WIN - Q_SUB) - REL_CLIP
    n_high = E_LEN - n_low - rel_size
    e = jnp.concatenate([
        jnp.broadcast_to(rel_bias[:, :1], (heads, n_low)),
        rel_bias,
        jnp.broadcast_to(rel_bias[:, -1:], (heads, n_high)),
    ], axis=1)
    return e.reshape(heads, 1, E_LEN).astype(F32)


def kernel(x, c, w_ada, b_ada, ffn1_w_in, ffn1_w_out, ln1_g, ln1_b, w_mix_in, rel_bias, w_alpha2, b_alpha,
           gla_norm_g, w_proj_a, w_proj_b, w_mix_out, ln2_g, ln2_b, ffn2_w_in, ffn2_w_out, ln3_g, ln3_b):
    batch, seq, d = x.shape
    depth = w_ada.shape[0]
    alpha = (2.0 * depth) ** 0.25
    kdim = w_alpha2.shape[2]
    vdim = w_proj_b.shape[1]
    n_main = 3 * A_WIDTH + 2 * kdim + 2 * vdim
    row = lambda v: v.reshape(1, -1)

    h = x.reshape(batch * seq, d)
    c_pad = jnp.zeros((16, d), F32).at[:batch].set(c)
    for l in range(depth):
        mod = _mod_call(c_pad, w_ada[l], row(b_ada[l]))[:batch].reshape(batch, N_MOD, d)

        h = _ffn_call(h, mod, ffn1_w_in[l].astype(BF16), ffn1_w_out[l].astype(BF16), row(ln1_g[l]), row(ln1_b[l]),
                      (0, 1, 2), alpha, seq)

        w_main_t, w_lr_t = _repack_call(jnp.swapaxes(w_mix_in, 1, 2)[l], n_main)
        p, lr = _mix_in_call(h, mod, w_main_t, w_lr_t, seq)

        ya, (w_pa, w_pb, w_mo, w2_in, w2_out) = _attn_call(
            p, _bias_table_call(_extended_rel_bias(rel_bias[l])), batch, seq,
            (w_proj_a[l], w_proj_b[l], w_mix_out[l], ffn2_w_in[l], ffn2_w_out[l]))
        w_a2 = jnp.zeros((LANES, kdim), BF16).at[:GATE_RANK].set(w_alpha2[l].astype(BF16))
        yb = _gla_call(p, lr, w_a2, row(b_alpha[l]), row(gla_norm_g[l]), batch, seq, kdim, vdim)

        h = _merge_call(ya, yb, p, h, mod, w_pa, w_pb, w_mo, row(ln2_g[l]), row(ln2_b[l]), alpha, seq)

        h = _ffn_call(h, mod, w2_in, w2_out, row(ln3_g[l]), row(ln3_b[l]), (6, 7, 8), alpha, seq)
    return h.reshape(batch, seq, d)
```

```python
import functools

import jax
import jax.numpy as jnp
from jax import lax
from jax.experimental import pallas as pl
from jax.experimental.pallas import tpu as pltpu

F32 = jnp.float32
BF16 = jnp.bfloat16

CHUNK = 64
A_HEADS = 16
A_HEAD_DIM = 64
A_WIDTH = A_HEADS * A_HEAD_DIM
A_PAST_CHUNKS = 8
REL_CLIP = 256
B_HEADS = 4
GATE_RANK = 16
GATE_TAU = 16.0
N_MOD = 9
LN_EPS = 1e-5
RMS_EPS = 1e-6

LANES = 128
SUBLANES = 8
VMEM_BYTES = 64 * 1024 * 1024
VMEM_LIMIT_BYTES = 56 * 1024 * 1024
FFN_VMEM_LIMIT_BYTES = VMEM_BYTES - 2 * 1024 * 1024

FFN_RING = 2
FFN_OUT_CHUNKS = 4

Q_CHUNKS = 4
Q_SUB = Q_CHUNKS * CHUNK
K_WIN = (A_PAST_CHUNKS + Q_CHUNKS) * CHUNK
ATTN_UNROLL = 14
E_LEN = Q_SUB + K_WIN


def _params(*sem, vmem_limit_bytes=VMEM_LIMIT_BYTES):
    return pltpu.CompilerParams(dimension_semantics=sem, vmem_limit_bytes=vmem_limit_bytes)


def _silu(v):
    return v * jax.nn.sigmoid(v)


def _layer_norm(y, g, b):
    mu = jnp.mean(y, axis=-1, keepdims=True)
    d = y - mu
    var = jnp.mean(d * d, axis=-1, keepdims=True)
    return d * lax.rsqrt(var + LN_EPS) * g + b


def _mod_kernel(c_ref, w_ref, b_ref, o_ref):
    s = _silu(c_ref[...]).astype(BF16)
    o_ref[...] = jnp.dot(s, w_ref[...].astype(BF16), preferred_element_type=F32) + b_ref[...]


def _mod_call(c_pad, w_ada, b_ada, tn=1024):
    rows, d = c_pad.shape
    n = w_ada.shape[1]
    return pl.pallas_call(
        _mod_kernel,
        grid=(n // tn,),
        in_specs=[
            pl.BlockSpec((rows, d), lambda j: (0, 0)),
            pl.BlockSpec((d, tn), lambda j: (0, j)),
            pl.BlockSpec((1, tn), lambda j: (0, j)),
        ],
        out_specs=pl.BlockSpec((rows, tn), lambda j: (0, j)),
        out_shape=jax.ShapeDtypeStruct((rows, n), F32),
        compiler_params=_params("parallel"),
        name="mod",
    )(c_pad, w_ada, b_ada)


def _ffn_kernel(h_ref, mod_ref, g_ref, b_ref, win_hbm, wout_hbm, o_ref, u_ref, wa_buf, wb_buf, wo_buf, sem,
                *, rows, alpha, out_chunks, n_tiles, nj):
    sh, sc, gt = rows
    tf = wo_buf.shape[1]
    base = pl.program_id(0) * nj
    total = n_tiles * nj

    def copies(s):
        j = lax.rem(s, nj)
        slot = lax.rem(s, FFN_RING)
        col_a = pl.multiple_of(j * tf, tf)
        col_b = pl.multiple_of((nj + j) * tf, tf)
        return (
            pltpu.make_async_copy(win_hbm.at[:, pl.ds(col_a, tf)], wa_buf.at[slot], sem.at[0, slot]),
            pltpu.make_async_copy(win_hbm.at[:, pl.ds(col_b, tf)], wb_buf.at[slot], sem.at[1, slot]),
            pltpu.make_async_copy(wout_hbm.at[pl.ds(col_a, tf), :], wo_buf.at[slot], sem.at[2, slot]),
        )

    def start(s):
        for c in copies(s):
            c.start()

    @pl.when(pl.program_id(0) == 0)
    def _():
        for s in range(FFN_RING - 1):
            start(s)

    def begin_step(s):
        @pl.when(s + (FFN_RING - 1) < total)
        def _():
            start(s + (FFN_RING - 1))

        for c in copies(s):
            c.wait()
        return lax.rem(s, FFN_RING)

    def gated(u, slot):
        a = jnp.dot(u, wa_buf[slot], preferred_element_type=F32)
        b = jnp.dot(u, wb_buf[slot], preferred_element_type=F32)
        return (_silu(a) * b).astype(BF16)

    slot = begin_step(base)
    u = (h_ref[...] * (1.0 + mod_ref[0, sc:sc + 1, :]) + mod_ref[0, sh:sh + 1, :]).astype(BF16)
    u_ref[...] = u
    o_ref[...] = jnp.dot(gated(u, slot), wo_buf[slot], preferred_element_type=F32)

    def middle(j, carry):
        slot = begin_step(base + j)
        o_ref[...] += jnp.dot(gated(u_ref[...], slot), wo_buf[slot], preferred_element_type=F32)
        return carry

    lax.fori_loop(1, nj - 1, middle, 0)

    slot = begin_step(base + (nj - 1))
    g = gated(u_ref[...], slot)
    tr = o_ref.shape[0] // out_chunks
    for r in range(out_chunks):
        rs = slice(r * tr, (r + 1) * tr)
        f = o_ref[rs, :] + jnp.dot(g[rs], wo_buf[slot], preferred_element_type=F32)
        y = alpha * h_ref[rs, :] + (0.5 * mod_ref[0, gt:gt + 1, :]) * f
        o_ref[rs, :] = _layer_norm(y, g_ref[...], b_ref[...])


def _ffn_call(h, mod, w_in, w_out, ln_g, ln_b, rows, alpha, seq, tm=1024, tf=512):
    n, d = h.shape
    dff = w_out.shape[0]
    nj = dff // tf
    per_batch = seq // tm
    return pl.pallas_call(
        functools.partial(_ffn_kernel, rows=rows, alpha=alpha, out_chunks=FFN_OUT_CHUNKS, n_tiles=n // tm, nj=nj),
        grid=(n // tm,),
        in_specs=[
            pl.BlockSpec((tm, d), lambda i: (i, 0)),
            pl.BlockSpec((1, N_MOD, d), lambda i: (i // per_batch, 0, 0)),
            pl.BlockSpec((1, d), lambda i: (0, 0)),
            pl.BlockSpec((1, d), lambda i: (0, 0)),
            pl.BlockSpec(memory_space=pl.ANY),
            pl.BlockSpec(memory_space=pl.ANY),
        ],
        out_specs=pl.BlockSpec((tm, d), lambda i: (i, 0)),
        out_shape=jax.ShapeDtypeStruct((n, d), F32),
        scratch_shapes=[
            pltpu.VMEM((tm, d), BF16),
            pltpu.VMEM((FFN_RING, d, tf), BF16),
            pltpu.VMEM((FFN_RING, d, tf), BF16),
            pltpu.VMEM((FFN_RING, tf, d), BF16),
            pltpu.SemaphoreType.DMA((3, FFN_RING)),
        ],
        compiler_params=_params("arbitrary", vmem_limit_bytes=FFN_VMEM_LIMIT_BYTES),
        name="ffn",
    )(h, mod, ln_g, ln_b, w_in, w_out)


def _repack_kernel(w_ref, wlr_ref, main_ref, lr_ref):
    main_ref[...] = w_ref[...].astype(BF16)
    row = lax.broadcasted_iota(jnp.int32, lr_ref.shape, 0)
    lr_ref[...] = jnp.where(row < GATE_RANK, wlr_ref[...], 0.0).astype(BF16)


def _repack_call(w_t, n_main, tr=1024):
    rows, d = w_t.shape
    main_blocks = n_main // tr
    src = lambda k: (SUBLANES * (k * (tr // SUBLANES) + jnp.where(k < main_blocks, 0, GATE_RANK // SUBLANES)), 0)
    return pl.pallas_call(
        _repack_kernel,
        grid=((rows - GATE_RANK) // tr,),
        in_specs=[
            pl.BlockSpec((pl.Element(tr), pl.Element(d)), src),
            pl.BlockSpec((pl.Element(LANES), pl.Element(d)), lambda k: (n_main, 0)),
        ],
        out_specs=[
            pl.BlockSpec((tr, d), lambda k: (k, 0)),
            pl.BlockSpec((LANES, d), lambda k: (0, 0)),
        ],
        out_shape=[
            jax.ShapeDtypeStruct((rows - GATE_RANK, d), BF16),
            jax.ShapeDtypeStruct((LANES, d), BF16),
        ],
        compiler_params=_params("arbitrary"),
        name="repack",
    )(w_t, w_t)


_NT = (((1,), (1,)), ((), ()))


def _mix_in_kernel(h_ref, mod_ref, w_ref, wlr_ref, p_ref, lr_ref, u_ref):
    j = pl.program_id(1)

    @pl.when(j == 0)
    def _():
        u = (h_ref[...] * (1.0 + mod_ref[0, 4:5, :]) + mod_ref[0, 3:4, :]).astype(BF16)
        u_ref[...] = u
        p_ref[...] = lax.dot_general(u, w_ref[...], _NT, preferred_element_type=F32).astype(BF16)
        lr_ref[...] = lax.dot_general(u, wlr_ref[...], _NT, preferred_element_type=F32)

    @pl.when(j > 0)
    def _():
        p_ref[...] = lax.dot_general(u_ref[...], w_ref[...], _NT, preferred_element_type=F32).astype(BF16)


def _mix_in_call(h, mod, w_main_t, w_lr_t, seq, tm=1024, tn=2048):
    n, d = h.shape
    cols = w_main_t.shape[0]
    per_batch = seq // tm
    return pl.pallas_call(
        _mix_in_kernel,
        grid=(n // tm, cols // tn),
        in_specs=[
            pl.BlockSpec((tm, d), lambda i, j: (i, 0)),
            pl.BlockSpec((1, N_MOD, d), lambda i, j: (i // per_batch, 0, 0)),
            pl.BlockSpec((tn, d), lambda i, j: (j, 0)),
            pl.BlockSpec((LANES, d), lambda i, j: (0, 0)),
        ],
        out_specs=[
            pl.BlockSpec((tm, tn), lambda i, j: (i, j)),
            pl.BlockSpec((tm, LANES), lambda i, j: (i, 0)),
        ],
        out_shape=[
            jax.ShapeDtypeStruct((n, cols), BF16),
            jax.ShapeDtypeStruct((n, LANES), F32),
        ],
        scratch_shapes=[pltpu.VMEM((tm, d), BF16)],
        compiler_params=_params("parallel", "arbitrary"),
        name="mix_in",
    )(h, mod, w_main_t, w_lr_t)


def _bias_table_kernel(e_ref, t_ref):
    row = lax.broadcasted_iota(jnp.int32, (Q_SUB, K_WIN), 0)
    col = lax.broadcasted_iota(jnp.int32, (Q_SUB, K_WIN), 1)
    q_chunk = row // CHUNK
    k_chunk = col // CHUNK
    in_band = (k_chunk >= q_chunk) & (k_chunk <= q_chunk + A_PAST_CHUNKS)
    for h in range(2):
        e = jnp.broadcast_to(e_ref[h], (Q_SUB, E_LEN))
        t = pltpu.roll(e, E_LEN - (Q_SUB - 1), 1, stride=1, stride_axis=0)[:, :K_WIN]
        t_ref[0, h * Q_SUB:(h + 1) * Q_SUB, :] = jnp.where(in_band, t, -jnp.inf)


def _bias_table_call(e_ext):
    pairs = e_ext.shape[0] // 2
    return pl.pallas_call(
        _bias_table_kernel,
        grid=(pairs,),
        in_specs=[pl.BlockSpec((2, 1, E_LEN), lambda hp: (hp, 0, 0))],
        out_specs=pl.BlockSpec((1, 2 * Q_SUB, K_WIN), lambda hp: (hp, 0, 0)),
        out_shape=jax.ShapeDtypeStruct((pairs, 2 * Q_SUB, K_WIN), F32),
        compiler_params=_params("parallel"),
        name="bias_table",
    )(e_ext)


def _attn_kernel(q_ref, k_ref, v_ref, t_ref, *refs, seq):
    n_cast = len(refs) // 2
    o_ref = refs[n_cast]
    for src_ref, dst_ref in zip(refs[:n_cast], refs[n_cast + 1:]):
        dst_ref[...] = src_ref[...].astype(dst_ref.dtype)

    lane = lax.broadcasted_iota(jnp.int32, (Q_SUB, 2 * A_HEAD_DIM), 1)
    head0 = lane < A_HEAD_DIM

    def sub_tile(q_start, k_start, width, t_start):
        q = q_ref[pl.ds(q_start, Q_SUB), :].astype(F32) * (A_HEAD_DIM ** -0.5)
        q2 = jnp.concatenate([jnp.where(head0, q, 0.0), jnp.where(head0, 0.0, q)], axis=0).astype(BF16)
        kw = k_ref[pl.ds(k_start, width), :]
        vw = v_ref[pl.ds(k_start, width), :]
        s = lax.dot_general(q2, kw, (((1,), (1,)), ((), ())), preferred_element_type=F32)
        s = s + t_ref[0, :, t_start:t_start + width]
        m = jnp.max(s, axis=-1, keepdims=True)
        p = jnp.exp(s - m)
        l = jnp.sum(p, axis=-1, keepdims=True)
        o = jnp.dot(p.astype(BF16), vw, preferred_element_type=F32) / l
        o_ref[pl.ds(q_start, Q_SUB), :] = jnp.where(head0, o[:Q_SUB], o[Q_SUB:]).astype(o_ref.dtype)

    n_head = (K_WIN - Q_SUB) // Q_SUB
    for j in range(n_head):
        sub_tile(j * Q_SUB, 0, (j + 1) * Q_SUB, K_WIN - (j + 1) * Q_SUB)

    def body(j, carry):
        q_start = pl.multiple_of(j * Q_SUB, Q_SUB)
        k_start = pl.multiple_of(j * Q_SUB - (K_WIN - Q_SUB), Q_SUB)
        sub_tile(q_start, k_start, K_WIN, 0)
        return carry

    lax.fori_loop(n_head, seq // Q_SUB, body, 0, unroll=ATTN_UNROLL)


def _attn_call(p, table, batch, seq, cast_weights):
    n = p.shape[0]
    pairs = A_HEADS // 2
    w = 2 * A_HEAD_DIM
    steps = batch * pairs
    slab = lambda a: pl.BlockSpec((a.shape[0] // steps, a.shape[1]), lambda b, hp: (b * pairs + hp, 0))
    outs = pl.pallas_call(
        functools.partial(_attn_kernel, seq=seq),
        grid=(batch, pairs),
        in_specs=[
            pl.BlockSpec((seq, w), lambda b, hp: (b, hp)),
            pl.BlockSpec((seq, w), lambda b, hp: (b, pairs + hp)),
            pl.BlockSpec((seq, w), lambda b, hp: (b, 2 * pairs + hp)),
            pl.BlockSpec((1, 2 * Q_SUB, K_WIN), lambda b, hp: (hp, 0, 0)),
        ] + [slab(a) for a in cast_weights],
        out_specs=[pl.BlockSpec((seq, w), lambda b, hp: (b, hp))] + [slab(a) for a in cast_weights],
        out_shape=[jax.ShapeDtypeStruct((n, A_WIDTH), BF16)]
        + [jax.ShapeDtypeStruct(a.shape, BF16) for a in cast_weights],
        compiler_params=_params("parallel", "parallel"),
        name="attn",
    )(p, p, p, table, *cast_weights)
    return outs[0], outs[1:]


def _gla_kernel(q_ref, k_ref, v_ref, r_ref, lr_ref, wa_ref, ba_ref, gn_ref, o_ref, st_ref, readout_ref,
                *, chunks, hk, hv):
    @pl.when(pl.program_id(1) == 0)
    def _():
        st_ref[...] = jnp.zeros_like(st_ref)

    rows = chunks * CHUNK
    z = jnp.dot(lr_ref[...].astype(BF16), wa_ref[...], preferred_element_type=F32) + ba_ref[...]
    log_a = -(jnp.maximum(-z, 0.0) + jnp.log(1.0 + jnp.exp(-jnp.abs(z)))) / GATE_TAU
    x0 = log_a.astype(BF16)
    x1 = (log_a - x0.astype(F32)).astype(BF16)
    ri = lax.broadcasted_iota(jnp.int32, (rows, rows), 0)
    ci = lax.broadcasted_iota(jnp.int32, (rows, rows), 1)
    later = jnp.logical_and(ri // CHUNK == ci // CHUNK, ci > ri).astype(BF16)
    rest = jnp.dot(later, x0, preferred_element_type=F32) + jnp.dot(later, x1, preferred_element_type=F32)
    rn = lax.broadcasted_iota(jnp.int32, (chunks, rows), 0)
    cn = lax.broadcasted_iota(jnp.int32, (chunks, rows), 1)
    member = (cn // CHUNK == rn).astype(BF16)
    total = jnp.dot(member, x0, preferred_element_type=F32) + jnp.dot(member, x1, preferred_element_type=F32)
    decay = jnp.exp(total)
    kdec = (k_ref[...].astype(F32) * jnp.exp(rest)).astype(BF16)

    for n in range(chunks):
        rs = slice(n * CHUNK, (n + 1) * CHUNK)
        for h in range(B_HEADS):
            ks = slice(h * hk, (h + 1) * hk)
            vs = slice(h * hv, (h + 1) * hv)
            upd = lax.dot_general(v_ref[rs, vs], kdec[rs, ks], (((0,), (0,)), ((), ())),
                                  preferred_element_type=F32)
            st = st_ref[h] * decay[n:n + 1, ks] + upd
            st_ref[h] = st
            readout_ref[rs, vs] = lax.dot_general(q_ref[rs, ks], st.astype(BF16), _NT, preferred_element_type=F32)

    for h in range(B_HEADS):
        vs = slice(h * hv, (h + 1) * hv)
        o = readout_ref[:, vs]
        o = o * lax.rsqrt(jnp.mean(o * o, axis=-1, keepdims=True) + RMS_EPS * hk) * gn_ref[...]
        o_ref[:, vs] = (o * _silu(r_ref[:, vs].astype(F32))).astype(o_ref.dtype)


def _gla_call(p, lr, w_alpha2, b_alpha, gn, batch, seq, kdim, vdim, rows=512):
    n = p.shape[0]
    steps = seq // rows
    hk, hv = kdim // B_HEADS, vdim // B_HEADS
    q_blk = (3 * A_WIDTH) // kdim
    v_blk = (3 * A_WIDTH + 2 * kdim) // vdim
    row_map = lambda col: (lambda b, t: (b * steps + t, col))
    return pl.pallas_call(
        functools.partial(_gla_kernel, chunks=rows // CHUNK, hk=hk, hv=hv),
        grid=(batch, steps),
        in_specs=[
            pl.BlockSpec((rows, kdim), row_map(q_blk)),
            pl.BlockSpec((rows, kdim), row_map(q_blk + 1)),
            pl.BlockSpec((rows, vdim), row_map(v_blk)),
            pl.BlockSpec((rows, vdim), row_map(v_blk + 1)),
            pl.BlockSpec((rows, LANES), row_map(0)),
            pl.BlockSpec((LANES, kdim), lambda b, t: (0, 0)),
            pl.BlockSpec((1, kdim), lambda b, t: (0, 0)),
            pl.BlockSpec((1, hv), lambda b, t: (0, 0)),
        ],
        out_specs=pl.BlockSpec((rows, vdim), row_map(0)),
        out_shape=jax.ShapeDtypeStruct((n, vdim), BF16),
        scratch_shapes=[pltpu.VMEM((B_HEADS, hv, hk), F32), pltpu.VMEM((rows, vdim), F32)],
        compiler_params=_params("parallel", "arbitrary"),
        name="gla",
    )(p, p, p, p, lr, w_alpha2, b_alpha, gn)


def _merge_kernel(ya_ref, yb_ref, ga_ref, gb_ref, h_ref, mod_ref, wpa_ref, wpb_ref, wo_ref, g_ref, b_ref, o_ref,
                  *, alpha, chunks):
    tr = o_ref.shape[0] // chunks
    for r in range(chunks):
        rs = slice(r * tr, (r + 1) * tr)
        pa = jnp.dot(ya_ref[rs, :], wpa_ref[...], preferred_element_type=F32)
        pb = jnp.dot(yb_ref[rs, :], wpb_ref[...], preferred_element_type=F32)
        merged = (jax.nn.sigmoid(ga_ref[rs, :].astype(F32)) * pa
                  + jax.nn.sigmoid(gb_ref[rs, :].astype(F32)) * pb)
        m = jnp.dot(merged.astype(BF16), wo_ref[...], preferred_element_type=F32)
        y = alpha * h_ref[rs, :] + mod_ref[0, 5:6, :] * m
        o_ref[rs, :] = _layer_norm(y, g_ref[...], b_ref[...])


def _merge_call(ya, yb, p, h, mod, w_pa, w_pb, w_out, ln_g, ln_b, alpha, seq, tm=512, chunks=2):
    n, d = h.shape
    per_batch = seq // tm
    ga_blk = (p.shape[1] - 2 * d) // d
    const = lambda shape: pl.BlockSpec(shape, lambda i: (0, 0), pipeline_mode=pl.Buffered(1))
    return pl.pallas_call(
        functools.partial(_merge_kernel, alpha=alpha, chunks=chunks),
        grid=(n // tm,),
        in_specs=[
            pl.BlockSpec((tm, ya.shape[1]), lambda i: (i, 0)),
            pl.BlockSpec((tm, yb.shape[1]), lambda i: (i, 0)),
            pl.BlockSpec((tm, d), lambda i: (i, ga_blk)),
            pl.BlockSpec((tm, d), lambda i: (i, ga_blk + 1)),
            pl.BlockSpec((tm, d), lambda i: (i, 0)),
            pl.BlockSpec((1, N_MOD, d), lambda i: (i // per_batch, 0, 0)),
            const(w_pa.shape),
            const(w_pb.shape),
            const(w_out.shape),
            const((1, d)),
            const((1, d)),
        ],
        out_specs=pl.BlockSpec((tm, d), lambda i: (i, 0)),
        out_shape=jax.ShapeDtypeStruct((n, d), F32),
        compiler_params=_params("parallel"),
        name="merge",
    )(ya, yb, p, p, h, mod, w_pa, w_pb, w_out, ln_g, ln_b)


def _extended_rel_bias(rel_bias):
    heads, rel_size = rel_bias.shape
    n_low = (Q_SUB - 1) + (K_WIN - Q_SUB) - REL_CLIP
    n_high = E_LEN - n_low - rel_size
    e = jnp.concatenate([
        jnp.broadcast_to(rel_bias[:, :1], (heads, n_low)),
        rel_bias,
        jnp.broadcast_to(rel_bias[:, -1:], (heads, n_high)),
    ], axis=1)
    return e.reshape(heads, 1, E_LEN).astype(F32)


def kernel(x, c, w_ada, b_ada, ffn1_w_in, ffn1_w_out, ln1_g, ln1_b, w_mix_in, rel_bias, w_alpha2, b_alpha,
           gla_norm_g, w_proj_a, w_proj_b, w_mix_out, ln2_g, ln2_b, ffn2_w_in, ffn2_w_out, ln3_g, ln3_b):
    batch, seq, d = x.shape
    depth = w_ada.shape[0]
    alpha = (2.0 * depth) ** 0.25
    kdim = w_alpha2.shape[2]
    vdim = w_proj_b.shape[1]
    n_main = 3 * A_WIDTH + 2 * kdim + 2 * vdim
    row = lambda v: v.reshape(1, -1)

    h = x.reshape(batch * seq, d)
    c_pad = jnp.zeros((16, d), F32).at[:batch].set(c)
    for l in range(depth):
        mod = _mod_call(c_pad, w_ada[l], row(b_ada[l]))[:batch].reshape(batch, N_MOD, d)

        h = _ffn_call(h, mod, ffn1_w_in[l].astype(BF16), ffn1_w_out[l].astype(BF16), row(ln1_g[l]), row(ln1_b[l]),
                      (0, 1, 2), alpha, seq)

        w_main_t, w_lr_t = _repack_call(jnp.swapaxes(w_mix_in, 1, 2)[l], n_main)
        p, lr = _mix_in_call(h, mod, w_main_t, w_lr_t, seq)

        ya, (w_pa, w_pb, w_mo, w2_in, w2_out) = _attn_call(
            p, _bias_table_call(_extended_rel_bias(rel_bias[l])), batch, seq,
            (w_proj_a[l], w_proj_b[l], w_mix_out[l], ffn2_w_in[l], ffn2_w_out[l]))
        w_a2 = jnp.zeros((LANES, kdim), BF16).at[:GATE_RANK].set(w_alpha2[l].astype(BF16))
        yb = _gla_call(p, lr, w_a2, row(b_alpha[l]), row(gla_norm_g[l]), batch, seq, kdim, vdim)

        h = _merge_call(ya, yb, p, h, mod, w_pa, w_pb, w_mo, row(ln2_g[l]), row(ln2_b[l]), alpha, seq)

        h = _ffn_call(h, mod, w2_in, w2_out, row(ln3_g[l]), row(ln3_b[l]), (6, 7, 8), alpha, seq)
    return h.reshape(batch, seq, d)
```

```python
import functools

import jax
import jax.numpy as jnp
from jax import lax
from jax.experimental import pallas as pl
from jax.experimental.pallas import tpu as pltpu

F32 = jnp.float32
BF16 = jnp.bfloat16

CHUNK = 64
A_HEADS = 16
A_HEAD_DIM = 64
A_WIDTH = A_HEADS * A_HEAD_DIM
A_PAST_CHUNKS = 8
REL_CLIP = 256
B_HEADS = 4
GATE_RANK = 16
GATE_TAU = 16.0
N_MOD = 9
LN_EPS = 1e-5
RMS_EPS = 1e-6

LANES = 128
SUBLANES = 8
VMEM_BYTES = 64 * 1024 * 1024
VMEM_LIMIT_BYTES = 56 * 1024 * 1024
FFN_VMEM_LIMIT_BYTES = VMEM_BYTES - 2 * 1024 * 1024

FFN_RING = 2
FFN_OUT_CHUNKS = 4

Q_CHUNKS = 4
Q_SUB = Q_CHUNKS * CHUNK
K_WIN = (A_PAST_CHUNKS + Q_CHUNKS) * CHUNK
ATTN_UNROLL = 14
E_LEN = Q_SUB + K_WIN


def _params(*sem, vmem_limit_bytes=VMEM_LIMIT_BYTES):
    return pltpu.CompilerParams(dimension_semantics=sem, vmem_limit_bytes=vmem_limit_bytes)


def _silu(v):
    return v * jax.nn.sigmoid(v)


def _layer_norm(y, g, b):
    mu = jnp.mean(y, axis=-1, keepdims=True)
    d = y - mu
    var = jnp.mean(d * d, axis=-1, keepdims=True)
    return d * lax.rsqrt(var + LN_EPS) * g + b


def _mod_kernel(c_ref, w_ref, b_ref, o_ref):
    s = _silu(c_ref[...]).astype(BF16)
    o_ref[...] = jnp.dot(s, w_ref[...].astype(BF16), preferred_element_type=F32) + b_ref[...]


def _mod_call(c_pad, w_ada, b_ada, tn=1024):
    rows, d = c_pad.shape
    n = w_ada.shape[1]
    return pl.pallas_call(
        _mod_kernel,
        grid=(n // tn,),
        in_specs=[
            pl.BlockSpec((rows, d), lambda j: (0, 0)),
            pl.BlockSpec((d, tn), lambda j: (0, j)),
            pl.BlockSpec((1, tn), lambda j: (0, j)),
        ],
        out_specs=pl.BlockSpec((rows, tn), lambda j: (0, j)),
        out_shape=jax.ShapeDtypeStruct((rows, n), F32),
        compiler_params=_params("parallel"),
        name="mod",
    )(c_pad, w_ada, b_ada)


def _ffn_kernel(h_ref, mod_ref, g_ref, b_ref, win_hbm, wout_hbm, o_ref, u_ref, wab_buf, wo_buf, sem,
                *, rows, alpha, out_chunks, n_tiles, nj):
    sh, sc, gt = rows
    tf = wo_buf.shape[1]
    base = pl.program_id(0) * nj
    total = n_tiles * nj

    def copies(s):
        slot = lax.rem(s, FFN_RING)
        col = pl.multiple_of(lax.rem(s, nj) * tf, tf)
        return (
            pltpu.make_async_copy(win_hbm.at[:, :, pl.ds(col, tf)], wab_buf.at[slot], sem.at[0, slot]),
            pltpu.make_async_copy(wout_hbm.at[pl.ds(col, tf), :], wo_buf.at[slot], sem.at[1, slot]),
        )

    def start(s):
        for c in copies(s):
            c.start()

    @pl.when(pl.program_id(0) == 0)
    def _():
        for s in range(FFN_RING - 1):
            start(s)

    def begin_step(s):
        @pl.when(s + (FFN_RING - 1) < total)
        def _():
            start(s + (FFN_RING - 1))

        for c in copies(s):
            c.wait()
        return lax.rem(s, FFN_RING)

    def gated(u, slot):
        a = jnp.dot(u, wab_buf[slot, 0], preferred_element_type=F32)
        b = jnp.dot(u, wab_buf[slot, 1], preferred_element_type=F32)
        return (_silu(a) * b).astype(BF16)

    slot = begin_step(base)
    u = (h_ref[...] * (1.0 + mod_ref[0, sc:sc + 1, :]) + mod_ref[0, sh:sh + 1, :]).astype(BF16)
    u_ref[...] = u
    o_ref[...] = jnp.dot(gated(u, slot), wo_buf[slot], preferred_element_type=F32)

    def middle(j, carry):
        slot = begin_step(base + j)
        o_ref[...] += jnp.dot(gated(u_ref[...], slot), wo_buf[slot], preferred_element_type=F32)
        return carry

    lax.fori_loop(1, nj - 1, middle, 0)

    slot = begin_step(base + (nj - 1))
    g = gated(u_ref[...], slot)
    tr = o_ref.shape[0] // out_chunks
    for r in range(out_chunks):
        rs = slice(r * tr, (r + 1) * tr)
        f = o_ref[rs, :] + jnp.dot(g[rs], wo_buf[slot], preferred_element_type=F32)
        y = alpha * h_ref[rs, :] + (0.5 * mod_ref[0, gt:gt + 1, :]) * f
        o_ref[rs, :] = _layer_norm(y, g_ref[...], b_ref[...])


def _ffn_call(h, mod, w_in, w_out, ln_g, ln_b, rows, alpha, seq, tm=1024, tf=512):
    n, d = h.shape
    dff = w_out.shape[0]
    nj = dff // tf
    per_batch = seq // tm
    return pl.pallas_call(
        functools.partial(_ffn_kernel, rows=rows, alpha=alpha, out_chunks=FFN_OUT_CHUNKS, n_tiles=n // tm, nj=nj),
        grid=(n // tm,),
        in_specs=[
            pl.BlockSpec((tm, d), lambda i: (i, 0)),
            pl.BlockSpec((1, N_MOD, d), lambda i: (i // per_batch, 0, 0)),
            pl.BlockSpec((1, d), lambda i: (0, 0)),
            pl.BlockSpec((1, d), lambda i: (0, 0)),
            pl.BlockSpec(memory_space=pl.ANY),
            pl.BlockSpec(memory_space=pl.ANY),
        ],
        out_specs=pl.BlockSpec((tm, d), lambda i: (i, 0)),
        out_shape=jax.ShapeDtypeStruct((n, d), F32),
        scratch_shapes=[
            pltpu.VMEM((tm, d), BF16),
            pltpu.VMEM((FFN_RING, 2, d, tf), BF16),
            pltpu.VMEM((FFN_RING, tf, d), BF16),
            pltpu.SemaphoreType.DMA((2, FFN_RING)),
        ],
        compiler_params=_params("arbitrary", vmem_limit_bytes=FFN_VMEM_LIMIT_BYTES),
        name="ffn",
    )(h, mod, ln_g, ln_b, w_in, w_out)


def _stack_halves_kernel(w_ref, o_ref):
    half = o_ref.shape[2]
    o_ref[0] = w_ref[:, :half].astype(o_ref.dtype)
    o_ref[1] = w_ref[:, half:].astype(o_ref.dtype)


def _stack_halves_call(w, tr=64):
    rows, cols = w.shape
    return pl.pallas_call(
        _stack_halves_kernel,
        grid=(rows // tr,),
        in_specs=[pl.BlockSpec((tr, cols), lambda i: (i, 0))],
        out_specs=pl.BlockSpec((2, tr, cols // 2), lambda i: (0, i, 0)),
        out_shape=jax.ShapeDtypeStruct((2, rows, cols // 2), BF16),
        compiler_params=_params("parallel"),
        name="stack_halves",
    )(w)


def _repack_kernel(w_ref, wlr_ref, main_ref, lr_ref):
    main_ref[...] = w_ref[...].astype(BF16)
    row = lax.broadcasted_iota(jnp.int32, lr_ref.shape, 0)
    lr_ref[...] = jnp.where(row < GATE_RANK, wlr_ref[...], 0.0).astype(BF16)


def _repack_call(w_t, n_main, tr=1024):
    rows, d = w_t.shape
    main_blocks = n_main // tr
    src = lambda k: (SUBLANES * (k * (tr // SUBLANES) + jnp.where(k < main_blocks, 0, GATE_RANK // SUBLANES)), 0)
    return pl.pallas_call(
        _repack_kernel,
        grid=((rows - GATE_RANK) // tr,),
        in_specs=[
            pl.BlockSpec((pl.Element(tr), pl.Element(d)), src),
            pl.BlockSpec((pl.Element(LANES), pl.Element(d)), lambda k: (n_main, 0)),
        ],
        out_specs=[
            pl.BlockSpec((tr, d), lambda k: (k, 0)),
            pl.BlockSpec((LANES, d), lambda k: (0, 0)),
        ],
        out_shape=[
            jax.ShapeDtypeStruct((rows - GATE_RANK, d), BF16),
            jax.ShapeDtypeStruct((LANES, d), BF16),
        ],
        compiler_params=_params("arbitrary"),
        name="repack",
    )(w_t, w_t)


_NT = (((1,), (1,)), ((), ()))


def _mix_in_kernel(h_ref, mod_ref, w_ref, wlr_ref, p_ref, lr_ref, u_ref):
    j = pl.program_id(1)

    @pl.when(j == 0)
    def _():
        u = (h_ref[...] * (1.0 + mod_ref[0, 4:5, :]) + mod_ref[0, 3:4, :]).astype(BF16)
        u_ref[...] = u
        p_ref[...] = lax.dot_general(u, w_ref[...], _NT, preferred_element_type=F32).astype(BF16)
        lr_ref[...] = lax.dot_general(u, wlr_ref[...], _NT, preferred_element_type=F32)

    @pl.when(j > 0)
    def _():
        p_ref[...] = lax.dot_general(u_ref[...], w_ref[...], _NT, preferred_element_type=F32).astype(BF16)


def _mix_in_call(h, mod, w_main_t, w_lr_t, seq, tm=1024, tn=2048):
    n, d = h.shape
    cols = w_main_t.shape[0]
    per_batch = seq // tm
    return pl.pallas_call(
        _mix_in_kernel,
        grid=(n // tm, cols // tn),
        in_specs=[
            pl.BlockSpec((tm, d), lambda i, j: (i, 0)),
            pl.BlockSpec((1, N_MOD, d), lambda i, j: (i // per_batch, 0, 0)),
            pl.BlockSpec((tn, d), lambda i, j: (j, 0)),
            pl.BlockSpec((LANES, d), lambda i, j: (0, 0)),
        ],
        out_specs=[
            pl.BlockSpec((tm, tn), lambda i, j: (i, j)),
            pl.BlockSpec((tm, LANES), lambda i, j: (i, 0)),
        ],
        out_shape=[
            jax.ShapeDtypeStruct((n, cols), BF16),
            jax.ShapeDtypeStruct((n, LANES), F32),
        ],
        scratch_shapes=[pltpu.VMEM((tm, d), BF16)],
        compiler_params=_params("parallel", "arbitrary"),
        name="mix_in",
    )(h, mod, w_main_t, w_lr_t)


def _bias_table_kernel(e_ref, t_ref):
    row = lax.broadcasted_iota(jnp.int32, (Q_SUB, K_WIN), 0)
    col = lax.broadcasted_iota(jnp.int32, (Q_SUB, K_WIN), 1)
    q_chunk = row // CHUNK
    k_chunk = col // CHUNK
    in_band = (k_chunk >= q_chunk) & (k_chunk <= q_chunk + A_PAST_CHUNKS)
    for h in range(2):
        e = jnp.broadcast_to(e_ref[h], (Q_SUB, E_LEN))
        t = pltpu.roll(e, E_LEN - (Q_SUB - 1), 1, stride=1, stride_axis=0)[:, :K_WIN]
        t_ref[0, h * Q_SUB:(h + 1) * Q_SUB, :] = jnp.where(in_band, t, -jnp.inf)


def _bias_table_call(e_ext):
    pairs = e_ext.shape[0] // 2
    return pl.pallas_call(
        _bias_table_kernel,
        grid=(pairs,),
        in_specs=[pl.BlockSpec((2, 1, E_LEN), lambda hp: (hp, 0, 0))],
        out_specs=pl.BlockSpec((1, 2 * Q_SUB, K_WIN), lambda hp: (hp, 0, 0)),
        out_shape=jax.ShapeDtypeStruct((pairs, 2 * Q_SUB, K_WIN), F32),
        compiler_params=_params("parallel"),
        name="bias_table",
    )(e_ext)


def _attn_kernel(q_ref, k_ref, v_ref, t_ref, *refs, seq):
    n_cast = len(refs) // 2
    o_ref = refs[n_cast]
    for src_ref, dst_ref in zip(refs[:n_cast], refs[n_cast + 1:]):
        if len(dst_ref.shape) == 3:
            half = dst_ref.shape[2]
            dst_ref[0] = src_ref[:, :half].astype(dst_ref.dtype)
            dst_ref[1] = src_ref[:, half:].astype(dst_ref.dtype)
        else:
            dst_ref[...] = src_ref[...].astype(dst_ref.dtype)

    lane = lax.broadcasted_iota(jnp.int32, (Q_SUB, 2 * A_HEAD_DIM), 1)
    head0 = lane < A_HEAD_DIM

    def sub_tile(q_start, k_start, width, t_start):
        q = q_ref[pl.ds(q_start, Q_SUB), :].astype(F32) * (A_HEAD_DIM ** -0.5)
        q2 = jnp.concatenate([jnp.where(head0, q, 0.0), jnp.where(head0, 0.0, q)], axis=0).astype(BF16)
        kw = k_ref[pl.ds(k_start, width), :]
        vw = v_ref[pl.ds(k_start, width), :]
        s = lax.dot_general(q2, kw, (((1,), (1,)), ((), ())), preferred_element_type=F32)
        s = s + t_ref[0, :, t_start:t_start + width]
        m = jnp.max(s, axis=-1, keepdims=True)
        p = jnp.exp(s - m)
        l = jnp.sum(p, axis=-1, keepdims=True)
        o = jnp.dot(p.astype(BF16), vw, preferred_element_type=F32) / l
        o_ref[pl.ds(q_start, Q_SUB), :] = jnp.where(head0, o[:Q_SUB], o[Q_SUB:]).astype(o_ref.dtype)

    n_head = (K_WIN - Q_SUB) // Q_SUB
    for j in range(n_head):
        sub_tile(j * Q_SUB, 0, (j + 1) * Q_SUB, K_WIN - (j + 1) * Q_SUB)

    def body(j, carry):
        q_start = pl.multiple_of(j * Q_SUB, Q_SUB)
        k_start = pl.multiple_of(j * Q_SUB - (K_WIN - Q_SUB), Q_SUB)
        sub_tile(q_start, k_start, K_WIN, 0)
        return carry

    lax.fori_loop(n_head, seq // Q_SUB, body, 0, unroll=ATTN_UNROLL)


def _attn_call(p, table, batch, seq, cast_weights, stack_halves):
    n = p.shape[0]
    pairs = A_HEADS // 2
    w = 2 * A_HEAD_DIM
    steps = batch * pairs
    slab = lambda a: pl.BlockSpec((a.shape[0] // steps, a.shape[1]), lambda b, hp: (b * pairs + hp, 0))

    def out_slab(a, stacked):
        if not stacked:
            return slab(a), jax.ShapeDtypeStruct(a.shape, BF16)
        rows, half = a.shape[0], a.shape[1] // 2
        return (pl.BlockSpec((2, rows // steps, half), lambda b, hp: (0, b * pairs + hp, 0)),
                jax.ShapeDtypeStruct((2, rows, half), BF16))

    cast_out = [out_slab(a, st) for a, st in zip(cast_weights, stack_halves)]
    outs = pl.pallas_call(
        functools.partial(_attn_kernel, seq=seq),
        grid=(batch, pairs),
        in_specs=[
            pl.BlockSpec((seq, w), lambda b, hp: (b, hp)),
            pl.BlockSpec((seq, w), lambda b, hp: (b, pairs + hp)),
            pl.BlockSpec((seq, w), lambda b, hp: (b, 2 * pairs + hp)),
            pl.BlockSpec((1, 2 * Q_SUB, K_WIN), lambda b, hp: (hp, 0, 0)),
        ] + [slab(a) for a in cast_weights],
        out_specs=[pl.BlockSpec((seq, w), lambda b, hp: (b, hp))] + [spec for spec, _ in cast_out],
        out_shape=[jax.ShapeDtypeStruct((n, A_WIDTH), BF16)] + [shape for _, shape in cast_out],
        compiler_params=_params("parallel", "parallel"),
        name="attn",
    )(p, p, p, table, *cast_weights)
    return outs[0], outs[1:]


def _gla_kernel(q_ref, k_ref, v_ref, r_ref, lr_ref, wa_ref, ba_ref, gn_ref, o_ref, st_ref, readout_ref,
                *, chunks, hk, hv):
    @pl.when(pl.program_id(1) == 0)
    def _():
        st_ref[...] = jnp.zeros_like(st_ref)

    rows = chunks * CHUNK
    z = jnp.dot(lr_ref[...].astype(BF16), wa_ref[...], preferred_element_type=F32) + ba_ref[...]
    log_a = -(jnp.maximum(-z, 0.0) + jnp.log(1.0 + jnp.exp(-jnp.abs(z)))) / GATE_TAU
    x0 = log_a.astype(BF16)
    x1 = (log_a - x0.astype(F32)).astype(BF16)
    ri = lax.broadcasted_iota(jnp.int32, (rows, rows), 0)
    ci = lax.broadcasted_iota(jnp.int32, (rows, rows), 1)
    later = jnp.logical_and(ri // CHUNK == ci // CHUNK, ci > ri).astype(BF16)
    rest = jnp.dot(later, x0, preferred_element_type=F32) + jnp.dot(later, x1, preferred_element_type=F32)
    rn = lax.broadcasted_iota(jnp.int32, (chunks, rows), 0)
    cn = lax.broadcasted_iota(jnp.int32, (chunks, rows), 1)
    member = (cn // CHUNK == rn).astype(BF16)
    total = jnp.dot(member, x0, preferred_element_type=F32) + jnp.dot(member, x1, preferred_element_type=F32)
    decay = jnp.exp(total)
    kdec = (k_ref[...].astype(F32) * jnp.exp(rest)).astype(BF16)

    for n in range(chunks):
        rs = slice(n * CHUNK, (n + 1) * CHUNK)
        for h in range(B_HEADS):
            ks = slice(h * hk, (h + 1) * hk)
            vs = slice(h * hv, (h + 1) * hv)
            upd = lax.dot_general(v_ref[rs, vs], kdec[rs, ks], (((0,), (0,)), ((), ())),
                                  preferred_element_type=F32)
            st = st_ref[h] * decay[n:n + 1, ks] + upd
            st_ref[h] = st
            readout_ref[rs, vs] = lax.dot_general(q_ref[rs, ks], st.astype(BF16), _NT, preferred_element_type=F32)

    for h in range(B_HEADS):
        vs = slice(h * hv, (h + 1) * hv)
        o = readout_ref[:, vs]
        o = o * lax.rsqrt(jnp.mean(o * o, axis=-1, keepdims=True) + RMS_EPS * hk) * gn_ref[...]
        o_ref[:, vs] = (o * _silu(r_ref[:, vs].astype(F32))).astype(o_ref.dtype)


def _gla_call(p, lr, w_alpha2, b_alpha, gn, batch, seq, kdim, vdim, rows=512):
    n = p.shape[0]
    steps = seq // rows
    hk, hv = kdim // B_HEADS, vdim // B_HEADS
    q_blk = (3 * A_WIDTH) // kdim
    v_blk = (3 * A_WIDTH + 2 * kdim) // vdim
    row_map = lambda col: (lambda b, t: (b * steps + t, col))
    return pl.pallas_call(
        functools.partial(_gla_kernel, chunks=rows // CHUNK, hk=hk, hv=hv),
        grid=(batch, steps),
        in_specs=[
            pl.BlockSpec((rows, kdim), row_map(q_blk)),
            pl.BlockSpec((rows, kdim), row_map(q_blk + 1)),
            pl.BlockSpec((rows, vdim), row_map(v_blk)),
            pl.BlockSpec((rows, vdim), row_map(v_blk + 1)),
            pl.BlockSpec((rows, LANES), row_map(0)),
            pl.BlockSpec((LANES, kdim), lambda b, t: (0, 0)),
            pl.BlockSpec((1, kdim), lambda b, t: (0, 0)),
            pl.BlockSpec((1, hv), lambda b, t: (0, 0)),
        ],
        out_specs=pl.BlockSpec((rows, vdim), row_map(0)),
        out_shape=jax.ShapeDtypeStruct((n, vdim), BF16),
        scratch_shapes=[pltpu.VMEM((B_HEADS, hv, hk), F32), pltpu.VMEM((rows, vdim), F32)],
        compiler_params=_params("parallel", "arbitrary"),
        name="gla",
    )(p, p, p, p, lr, w_alpha2, b_alpha, gn)


def _merge_kernel(ya_ref, yb_ref, ga_ref, gb_ref, h_ref, mod_ref, wpa_ref, wpb_ref, wo_ref, g_ref, b_ref, o_ref,
                  *, alpha, chunks):
    tr = o_ref.shape[0] // chunks
    for r in range(chunks):
        rs = slice(r * tr, (r + 1) * tr)
        pa = jnp.dot(ya_ref[rs, :], wpa_ref[...], preferred_element_type=F32)
        pb = jnp.dot(yb_ref[rs, :], wpb_ref[...], preferred_element_type=F32)
        merged = (jax.nn.sigmoid(ga_ref[rs, :].astype(F32)) * pa
                  + jax.nn.sigmoid(gb_ref[rs, :].astype(F32)) * pb)
        m = jnp.dot(merged.astype(BF16), wo_ref[...], preferred_element_type=F32)
        y = alpha * h_ref[rs, :] + mod_ref[0, 5:6, :] * m
        o_ref[rs, :] = _layer_norm(y, g_ref[...], b_ref[...])


def _merge_call(ya, yb, p, h, mod, w_pa, w_pb, w_out, ln_g, ln_b, alpha, seq, tm=512, chunks=2):
    n, d = h.shape
    per_batch = seq // tm
    ga_blk = (p.shape[1] - 2 * d) // d
    const = lambda shape: pl.BlockSpec(shape, lambda i: (0, 0), pipeline_mode=pl.Buffered(1))
    return pl.pallas_call(
        functools.partial(_merge_kernel, alpha=alpha, chunks=chunks),
        grid=(n // tm,),
        in_specs=[
            pl.BlockSpec((tm, ya.shape[1]), lambda i: (i, 0)),
            pl.BlockSpec((tm, yb.shape[1]), lambda i: (i, 0)),
            pl.BlockSpec((tm, d), lambda i: (i, ga_blk)),
            pl.BlockSpec((tm, d), lambda i: (i, ga_blk + 1)),
            pl.BlockSpec((tm, d), lambda i: (i, 0)),
            pl.BlockSpec((1, N_MOD, d), lambda i: (i // per_batch, 0, 0)),
            const(w_pa.shape),
            const(w_pb.shape),
            const(w_out.shape),
            const((1, d)),
            const((1, d)),
        ],
        out_specs=pl.BlockSpec((tm, d), lambda i: (i, 0)),
        out_shape=jax.ShapeDtypeStruct((n, d), F32),
        compiler_params=_params("parallel"),
        name="merge",
    )(ya, yb, p, p, h, mod, w_pa, w_pb, w_out, ln_g, ln_b)


def _extended_rel_bias(rel_bias):
    heads, rel_size = rel_bias.shape
    n_low = (Q_SUB - 1) + (K_WIN - Q_SUB) - REL_CLIP
    n_high = E_LEN - n_low - rel_size
    e = jnp.concatenate([
        jnp.broadcast_to(rel_bias[:, :1], (heads, n_low)),
        rel_bias,
        jnp.broadcast_to(rel_bias[:, -1:], (heads, n_high)),
    ], axis=1)
    return e.reshape(heads, 1, E_LEN).astype(F32)


def kernel(x, c, w_ada, b_ada, ffn1_w_in, ffn1_w_out, ln1_g, ln1_b, w_mix_in, rel_bias, w_alpha2, b_alpha,
           gla_norm_g, w_proj_a, w_proj_b, w_mix_out, ln2_g, ln2_b, ffn2_w_in, ffn2_w_out, ln3_g, ln3_b):
    batch, seq, d = x.shape
    depth = w_ada.shape[0]
    alpha = (2.0 * depth) ** 0.25
    kdim = w_alpha2.shape[2]
    vdim = w_proj_b.shape[1]
    n_main = 3 * A_WIDTH + 2 * kdim + 2 * vdim
    row = lambda v: v.reshape(1, -1)

    h = x.reshape(batch * seq, d)
    c_pad = jnp.zeros((16, d), F32).at[:batch].set(c)
    for l in range(depth):
        mod = _mod_call(c_pad, w_ada[l], row(b_ada[l]))[:batch].reshape(batch, N_MOD, d)

        h = _ffn_call(h, mod, _stack_halves_call(ffn1_w_in[l]), ffn1_w_out[l].astype(BF16), row(ln1_g[l]), row(ln1_b[l]),
                      (0, 1, 2), alpha, seq)

        w_main_t, w_lr_t = _repack_call(jnp.swapaxes(w_mix_in, 1, 2)[l], n_main)
        p, lr = _mix_in_call(h, mod, w_main_t, w_lr_t, seq)

        ya, (w_pa, w_pb, w_mo, w2_in, w2_out) = _attn_call(
            p, _bias_table_call(_extended_rel_bias(rel_bias[l])), batch, seq,
            (w_proj_a[l], w_proj_b[l], w_mix_out[l], ffn2_w_in[l], ffn2_w_out[l]), (False, False, False, True, False))
        w_a2 = jnp.zeros((LANES, kdim), BF16).at[:GATE_RANK].set(w_alpha2[l].astype(BF16))
        yb = _gla_call(p, lr, w_a2, row(b_alpha[l]), row(gla_norm_g[l]), batch, seq, kdim, vdim)

        h = _merge_call(ya, yb, p, h, mod, w_pa, w_pb, w_mo, row(ln2_g[l]), row(ln2_b[l]), alpha, seq)

        h = _ffn_call(h, mod, w2_in, w2_out, row(ln3_g[l]), row(ln3_b[l]), (6, 7, 8), alpha, seq)
    return h.reshape(batch, seq, d)
```

```python
import functools

import jax
import jax.numpy as jnp
from jax import lax
from jax.experimental import pallas as pl
from jax.experimental.pallas import tpu as pltpu

F32 = jnp.float32
BF16 = jnp.bfloat16

CHUNK = 64
A_HEADS = 16
A_HEAD_DIM = 64
A_WIDTH = A_HEADS * A_HEAD_DIM
A_PAST_CHUNKS = 8
REL_CLIP = 256
B_HEADS = 4
GATE_RANK = 16
GATE_TAU = 16.0
N_MOD = 9
LN_EPS = 1e-5
RMS_EPS = 1e-6

LANES = 128
SUBLANES = 8
VMEM_BYTES = 64 * 1024 * 1024
VMEM_LIMIT_BYTES = 56 * 1024 * 1024
FFN_VMEM_LIMIT_BYTES = VMEM_BYTES - 2 * 1024 * 1024

FFN_RING = 2
FFN_UNROLL = 3
FFN_OUT_CHUNKS = 4

Q_CHUNKS = 4
Q_SUB = Q_CHUNKS * CHUNK
K_WIN = (A_PAST_CHUNKS + Q_CHUNKS) * CHUNK
ATTN_UNROLL = 14
E_LEN = Q_SUB + K_WIN


def _params(*sem, vmem_limit_bytes=VMEM_LIMIT_BYTES):
    return pltpu.CompilerParams(dimension_semantics=sem, vmem_limit_bytes=vmem_limit_bytes)


def _silu(v):
    return v * jax.nn.sigmoid(v)


def _layer_norm(y, g, b):
    mu = jnp.mean(y, axis=-1, keepdims=True)
    d = y - mu
    var = jnp.mean(d * d, axis=-1, keepdims=True)
    return d * lax.rsqrt(var + LN_EPS) * g + b


def _mod_kernel(c_ref, w_ref, b_ref, o_ref):
    s = _silu(c_ref[...]).astype(BF16)
    o_ref[...] = jnp.dot(s, w_ref[...].astype(BF16), preferred_element_type=F32) + b_ref[...]


def _mod_call(c_pad, w_ada, b_ada, tn=1024):
    rows, d = c_pad.shape
    n = w_ada.shape[1]
    return pl.pallas_call(
        _mod_kernel,
        grid=(n // tn,),
        in_specs=[
            pl.BlockSpec((rows, d), lambda j: (0, 0)),
            pl.BlockSpec((d, tn), lambda j: (0, j)),
            pl.BlockSpec((1, tn), lambda j: (0, j)),
        ],
        out_specs=pl.BlockSpec((rows, tn), lambda j: (0, j)),
        out_shape=jax.ShapeDtypeStruct((rows, n), F32),
        compiler_params=_params("parallel"),
        name="mod",
    )(c_pad, w_ada, b_ada)


def _ffn_kernel(h_ref, mod_ref, g_ref, b_ref, win_hbm, wout_hbm, o_ref, u_ref, wab_buf, wo_buf, sem,
                *, rows, alpha, out_chunks, n_tiles, nj):
    sh, sc, gt = rows
    tf = wo_buf.shape[1]
    base = pl.program_id(0) * nj
    total = n_tiles * nj

    def copies(s):
        slot = lax.rem(s, FFN_RING)
        col = pl.multiple_of(lax.rem(s, nj) * tf, tf)
        return (
            pltpu.make_async_copy(win_hbm.at[:, :, pl.ds(col, tf)], wab_buf.at[slot], sem.at[0, slot]),
            pltpu.make_async_copy(wout_hbm.at[pl.ds(col, tf), :], wo_buf.at[slot], sem.at[1, slot]),
        )

    def start(s):
        for c in copies(s):
            c.start()

    @pl.when(pl.program_id(0) == 0)
    def _():
        for s in range(FFN_RING - 1):
            start(s)

    def begin_step(s):
        @pl.when(s + (FFN_RING - 1) < total)
        def _():
            start(s + (FFN_RING - 1))

        for c in copies(s):
            c.wait()
        return lax.rem(s, FFN_RING)

    def gated(u, slot):
        a = jnp.dot(u, wab_buf[slot, 0], preferred_element_type=F32)
        b = jnp.dot(u, wab_buf[slot, 1], preferred_element_type=F32)
        return (_silu(a) * b).astype(BF16)

    slot = begin_step(base)
    u = (h_ref[...] * (1.0 + mod_ref[0, sc:sc + 1, :]) + mod_ref[0, sh:sh + 1, :]).astype(BF16)
    u_ref[...] = u
    o_ref[...] = jnp.dot(gated(u, slot), wo_buf[slot], preferred_element_type=F32)

    def middle(j, carry):
        slot = begin_step(base + j)
        o_ref[...] += jnp.dot(gated(u_ref[...], slot), wo_buf[slot], preferred_element_type=F32)
        return carry

    lax.fori_loop(1, nj - 1, middle, 0, unroll=FFN_UNROLL)

    slot = begin_step(base + (nj - 1))
    g = gated(u_ref[...], slot)
    tr = o_ref.shape[0] // out_chunks
    for r in range(out_chunks):
        rs = slice(r * tr, (r + 1) * tr)
        f = o_ref[rs, :] + jnp.dot(g[rs], wo_buf[slot], preferred_element_type=F32)
        y = alpha * h_ref[rs, :] + (0.5 * mod_ref[0, gt:gt + 1, :]) * f
        o_ref[rs, :] = _layer_norm(y, g_ref[...], b_ref[...])


def _ffn_call(h, mod, w_in, w_out, ln_g, ln_b, rows, alpha, seq, tm=1024, tf=512):
    n, d = h.shape
    dff = w_out.shape[0]
    nj = dff // tf
    per_batch = seq // tm
    return pl.pallas_call(
        functools.partial(_ffn_kernel, rows=rows, alpha=alpha, out_chunks=FFN_OUT_CHUNKS, n_tiles=n // tm, nj=nj),
        grid=(n // tm,),
        in_specs=[
            pl.BlockSpec((tm, d), lambda i: (i, 0)),
            pl.BlockSpec((1, N_MOD, d), lambda i: (i // per_batch, 0, 0)),
            pl.BlockSpec((1, d), lambda i: (0, 0)),
            pl.BlockSpec((1, d), lambda i: (0, 0)),
            pl.BlockSpec(memory_space=pl.ANY),
            pl.BlockSpec(memory_space=pl.ANY),
        ],
        out_specs=pl.BlockSpec((tm, d), lambda i: (i, 0)),
        out_shape=jax.ShapeDtypeStruct((n, d), F32),
        scratch_shapes=[
            pltpu.VMEM((tm, d), BF16),
            pltpu.VMEM((FFN_RING, 2, d, tf), BF16),
            pltpu.VMEM((FFN_RING, tf, d), BF16),
            pltpu.SemaphoreType.DMA((2, FFN_RING)),
        ],
        compiler_params=_params("arbitrary", vmem_limit_bytes=FFN_VMEM_LIMIT_BYTES),
        name="ffn",
    )(h, mod, ln_g, ln_b, w_in, w_out)


def _stack_halves_kernel(w_ref, o_ref):
    half = o_ref.shape[2]
    o_ref[0] = w_ref[:, :half].astype(o_ref.dtype)
    o_ref[1] = w_ref[:, half:].astype(o_ref.dtype)


def _stack_halves_call(w, tr=64):
    rows, cols = w.shape
    return pl.pallas_call(
        _stack_halves_kernel,
        grid=(rows // tr,),
        in_specs=[pl.BlockSpec((tr, cols), lambda i: (i, 0))],
        out_specs=pl.BlockSpec((2, tr, cols // 2), lambda i: (0, i, 0)),
        out_shape=jax.ShapeDtypeStruct((2, rows, cols // 2), BF16),
        compiler_params=_params("parallel"),
        name="stack_halves",
    )(w)


def _repack_kernel(w_ref, wlr_ref, main_ref, lr_ref):
    main_ref[...] = w_ref[...].astype(BF16)
    row = lax.broadcasted_iota(jnp.int32, lr_ref.shape, 0)
    lr_ref[...] = jnp.where(row < GATE_RANK, wlr_ref[...], 0.0).astype(BF16)


def _repack_call(w_t, n_main, tr=1024):
    rows, d = w_t.shape
    main_blocks = n_main // tr
    src = lambda k: (SUBLANES * (k * (tr // SUBLANES) + jnp.where(k < main_blocks, 0, GATE_RANK // SUBLANES)), 0)
    return pl.pallas_call(
        _repack_kernel,
        grid=((rows - GATE_RANK) // tr,),
        in_specs=[
            pl.BlockSpec((pl.Element(tr), pl.Element(d)), src),
            pl.BlockSpec((pl.Element(LANES), pl.Element(d)), lambda k: (n_main, 0)),
        ],
        out_specs=[
            pl.BlockSpec((tr, d), lambda k: (k, 0)),
            pl.BlockSpec((LANES, d), lambda k: (0, 0)),
        ],
        out_shape=[
            jax.ShapeDtypeStruct((rows - GATE_RANK, d), BF16),
            jax.ShapeDtypeStruct((LANES, d), BF16),
        ],
        compiler_params=_params("arbitrary"),
        name="repack",
    )(w_t, w_t)


_NT = (((1,), (1,)), ((), ()))


def _mix_in_kernel(h_ref, mod_ref, w_ref, wlr_ref, p_ref, lr_ref, u_ref):
    j = pl.program_id(1)

    @pl.when(j == 0)
    def _():
        u = (h_ref[...] * (1.0 + mod_ref[0, 4:5, :]) + mod_ref[0, 3:4, :]).astype(BF16)
        u_ref[...] = u
        p_ref[...] = lax.dot_general(u, w_ref[...], _NT, preferred_element_type=F32).astype(BF16)
        lr_ref[...] = lax.dot_general(u, wlr_ref[...], _NT, preferred_element_type=F32)

    @pl.when(j > 0)
    def _():
        p_ref[...] = lax.dot_general(u_ref[...], w_ref[...], _NT, preferred_element_type=F32).astype(BF16)


def _mix_in_call(h, mod, w_main_t, w_lr_t, seq, tm=1024, tn=2048):
    n, d = h.shape
    cols = w_main_t.shape[0]
    per_batch = seq // tm
    return pl.pallas_call(
        _mix_in_kernel,
        grid=(n // tm, cols // tn),
        in_specs=[
            pl.BlockSpec((tm, d), lambda i, j: (i, 0)),
            pl.BlockSpec((1, N_MOD, d), lambda i, j: (i // per_batch, 0, 0)),
            pl.BlockSpec((tn, d), lambda i, j: (j, 0)),
            pl.BlockSpec((LANES, d), lambda i, j: (0, 0)),
        ],
        out_specs=[
            pl.BlockSpec((tm, tn), lambda i, j: (i, j)),
            pl.BlockSpec((tm, LANES), lambda i, j: (i, 0)),
        ],
        out_shape=[
            jax.ShapeDtypeStruct((n, cols), BF16),
            jax.ShapeDtypeStruct((n, LANES), F32),
        ],
        scratch_shapes=[pltpu.VMEM((tm, d), BF16)],
        compiler_params=_params("parallel", "arbitrary"),
        name="mix_in",
    )(h, mod, w_main_t, w_lr_t)


def _bias_table_kernel(e_ref, t_ref):
    row = lax.broadcasted_iota(jnp.int32, (Q_SUB, K_WIN), 0)
    col = lax.broadcasted_iota(jnp.int32, (Q_SUB, K_WIN), 1)
    q_chunk = row // CHUNK
    k_chunk = col // CHUNK
    in_band = (k_chunk >= q_chunk) & (k_chunk <= q_chunk + A_PAST_CHUNKS)
    for h in range(2):
        e = jnp.broadcast_to(e_ref[h], (Q_SUB, E_LEN))
        t = pltpu.roll(e, E_LEN - (Q_SUB - 1), 1, stride=1, stride_axis=0)[:, :K_WIN]
        t_ref[0, h * Q_SUB:(h + 1) * Q_SUB, :] = jnp.where(in_band, t, -jnp.inf)


def _bias_table_call(e_ext):
    pairs = e_ext.shape[0] // 2
    return pl.pallas_call(
        _bias_table_kernel,
        grid=(pairs,),
        in_specs=[pl.BlockSpec((2, 1, E_LEN), lambda hp: (hp, 0, 0))],
        out_specs=pl.BlockSpec((1, 2 * Q_SUB, K_WIN), lambda hp: (hp, 0, 0)),
        out_shape=jax.ShapeDtypeStruct((pairs, 2 * Q_SUB, K_WIN), F32),
        compiler_params=_params("parallel"),
        name="bias_table",
    )(e_ext)


def _attn_kernel(q_ref, k_ref, v_ref, t_ref, *refs, seq):
    n_cast = len(refs) // 2
    o_ref = refs[n_cast]
    for src_ref, dst_ref in zip(refs[:n_cast], refs[n_cast + 1:]):
        if len(dst_ref.shape) == 3:
            half = dst_ref.shape[2]
            dst_ref[0] = src_ref[:, :half].astype(dst_ref.dtype)
            dst_ref[1] = src_ref[:, half:].astype(dst_ref.dtype)
        else:
            dst_ref[...] = src_ref[...].astype(dst_ref.dtype)

    lane = lax.broadcasted_iota(jnp.int32, (Q_SUB, 2 * A_HEAD_DIM), 1)
    head0 = lane < A_HEAD_DIM

    def sub_tile(q_start, k_start, width, t_start):
        q = q_ref[pl.ds(q_start, Q_SUB), :].astype(F32) * (A_HEAD_DIM ** -0.5)
        q2 = jnp.concatenate([jnp.where(head0, q, 0.0), jnp.where(head0, 0.0, q)], axis=0).astype(BF16)
        kw = k_ref[pl.ds(k_start, width), :]
        vw = v_ref[pl.ds(k_start, width), :]
        s = lax.dot_general(q2, kw, (((1,), (1,)), ((), ())), preferred_element_type=F32)
        s = s + t_ref[0, :, t_start:t_start + width]
        m = jnp.max(s, axis=-1, keepdims=True)
        p = jnp.exp(s - m)
        l = jnp.sum(p, axis=-1, keepdims=True)
        o = jnp.dot(p.astype(BF16), vw, preferred_element_type=F32) / l
        o_ref[pl.ds(q_start, Q_SUB), :] = jnp.where(head0, o[:Q_SUB], o[Q_SUB:]).astype(o_ref.dtype)

    n_head = (K_WIN - Q_SUB) // Q_SUB
    for j in range(n_head):
        sub_tile(j * Q_SUB, 0, (j + 1) * Q_SUB, K_WIN - (j + 1) * Q_SUB)

    def body(j, carry):
        q_start = pl.multiple_of(j * Q_SUB, Q_SUB)
        k_start = pl.multiple_of(j * Q_SUB - (K_WIN - Q_SUB), Q_SUB)
        sub_tile(q_start, k_start, K_WIN, 0)
        return carry

    lax.fori_loop(n_head, seq // Q_SUB, body, 0, unroll=ATTN_UNROLL)


def _attn_call(p, table, batch, seq, cast_weights, stack_halves):
    n = p.shape[0]
    pairs = A_HEADS // 2
    w = 2 * A_HEAD_DIM
    steps = batch * pairs
    slab = lambda a: pl.BlockSpec((a.shape[0] // steps, a.shape[1]), lambda b, hp: (b * pairs + hp, 0))

    def out_slab(a, stacked):
        if not stacked:
            return slab(a), jax.ShapeDtypeStruct(a.shape, BF16)
        rows, half = a.shape[0], a.shape[1] // 2
        return (pl.BlockSpec((2, rows // steps, half), lambda b, hp: (0, b * pairs + hp, 0)),
                jax.ShapeDtypeStruct((2, rows, half), BF16))

    cast_out = [out_slab(a, st) for a, st in zip(cast_weights, stack_halves)]
    outs = pl.pallas_call(
        functools.partial(_attn_kernel, seq=seq),
        grid=(batch, pairs),
        in_specs=[
            pl.BlockSpec((seq, w), lambda b, hp: (b, hp)),
            pl.BlockSpec((seq, w), lambda b, hp: (b, pairs + hp)),
            pl.BlockSpec((seq, w), lambda b, hp: (b, 2 * pairs + hp)),
            pl.BlockSpec((1, 2 * Q_SUB, K_WIN), lambda b, hp: (hp, 0, 0)),
        ] + [slab(a) for a in cast_weights],
        out_specs=[pl.BlockSpec((seq, w), lambda b, hp: (b, hp))] + [spec for spec, _ in cast_out],
        out_shape=[jax.ShapeDtypeStruct((n, A_WIDTH), BF16)] + [shape for _, shape in cast_out],
        compiler_params=_params("parallel", "parallel"),
        name="attn",
    )(p, p, p, table, *cast_weights)
    return outs[0], outs[1:]


def _gla_kernel(q_ref, k_ref, v_ref, r_ref, lr_ref, wa_ref, ba_ref, gn_ref, o_ref, st_ref, readout_ref,
                *, chunks, hk, hv):
    @pl.when(pl.program_id(1) == 0)
    def _():
        st_ref[...] = jnp.zeros_like(st_ref)

    rows = chunks * CHUNK
    z = jnp.dot(lr_ref[...].astype(BF16), wa_ref[...], preferred_element_type=F32) + ba_ref[...]
    log_a = -(jnp.maximum(-z, 0.0) + jnp.log(1.0 + jnp.exp(-jnp.abs(z)))) / GATE_TAU
    x0 = log_a.astype(BF16)
    x1 = (log_a - x0.astype(F32)).astype(BF16)
    ri = lax.broadcasted_iota(jnp.int32, (rows, rows), 0)
    ci = lax.broadcasted_iota(jnp.int32, (rows, rows), 1)
    later = jnp.logical_and(ri // CHUNK == ci // CHUNK, ci > ri).astype(BF16)
    rest = jnp.dot(later, x0, preferred_element_type=F32) + jnp.dot(later, x1, preferred_element_type=F32)
    rn = lax.broadcasted_iota(jnp.int32, (chunks, rows), 0)
    cn = lax.broadcasted_iota(jnp.int32, (chunks, rows), 1)
    member = (cn // CHUNK == rn).astype(BF16)
    total = jnp.dot(member, x0, preferred_element_type=F32) + jnp.dot(member, x1, preferred_element_type=F32)
    decay = jnp.exp(total)
    kdec = (k_ref[...].astype(F32) * jnp.exp(rest)).astype(BF16)

    for n in range(chunks):
        rs = slice(n * CHUNK, (n + 1) * CHUNK)
        for h in range(B_HEADS):
            ks = slice(h * hk, (h + 1) * hk)
            vs = slice(h * hv, (h + 1) * hv)
            upd = lax.dot_general(v_ref[rs, vs], kdec[rs, ks], (((0,), (0,)), ((), ())),
                                  preferred_element_type=F32)
            st = st_ref[h] * decay[n:n + 1, ks] + upd
            st_ref[h] = st
            readout_ref[rs, vs] = lax.dot_general(q_ref[rs, ks], st.astype(BF16), _NT, preferred_element_type=F32)

    for h in range(B_HEADS):
        vs = slice(h * hv, (h + 1) * hv)
        o = readout_ref[:, vs]
        o = o * lax.rsqrt(jnp.mean(o * o, axis=-1, keepdims=True) + RMS_EPS * hk) * gn_ref[...]
        o_ref[:, vs] = (o * _silu(r_ref[:, vs].astype(F32))).astype(o_ref.dtype)


def _gla_call(p, lr, w_alpha2, b_alpha, gn, batch, seq, kdim, vdim, rows=512):
    n = p.shape[0]
    steps = seq // rows
    hk, hv = kdim // B_HEADS, vdim // B_HEADS
    q_blk = (3 * A_WIDTH) // kdim
    v_blk = (3 * A_WIDTH + 2 * kdim) // vdim
    row_map = lambda col: (lambda b, t: (b * steps + t, col))
    return pl.pallas_call(
        functools.partial(_gla_kernel, chunks=rows // CHUNK, hk=hk, hv=hv),
        grid=(batch, steps),
        in_specs=[
            pl.BlockSpec((rows, kdim), row_map(q_blk)),
            pl.BlockSpec((rows, kdim), row_map(q_blk + 1)),
            pl.BlockSpec((rows, vdim), row_map(v_blk)),
            pl.BlockSpec((rows, vdim), row_map(v_blk + 1)),
            pl.BlockSpec((rows, LANES), row_map(0)),
            pl.BlockSpec((LANES, kdim), lambda b, t: (0, 0)),
            pl.BlockSpec((1, kdim), lambda b, t: (0, 0)),
            pl.BlockSpec((1, hv), lambda b, t: (0, 0)),
        ],
        out_specs=pl.BlockSpec((rows, vdim), row_map(0)),
        out_shape=jax.ShapeDtypeStruct((n, vdim), BF16),
        scratch_shapes=[pltpu.VMEM((B_HEADS, hv, hk), F32), pltpu.VMEM((rows, vdim), F32)],
        compiler_params=_params("parallel", "arbitrary"),
        name="gla",
    )(p, p, p, p, lr, w_alpha2, b_alpha, gn)


def _merge_kernel(ya_ref, yb_ref, ga_ref, gb_ref, h_ref, mod_ref, wpa_ref, wpb_ref, wo_ref, g_ref, b_ref, o_ref,
                  *, alpha, chunks):
    tr = o_ref.shape[0] // chunks
    for r in range(chunks):
        rs = slice(r * tr, (r + 1) * tr)
        pa = jnp.dot(ya_ref[rs, :], wpa_ref[...], preferred_element_type=F32)
        pb = jnp.dot(yb_ref[rs, :], wpb_ref[...], preferred_element_type=F32)
        merged = (jax.nn.sigmoid(ga_ref[rs, :].astype(F32)) * pa
                  + jax.nn.sigmoid(gb_ref[rs, :].astype(F32)) * pb)
        m = jnp.dot(merged.astype(BF16), wo_ref[...], preferred_element_type=F32)
        y = alpha * h_ref[rs, :] + mod_ref[0, 5:6, :] * m
        o_ref[rs, :] = _layer_norm(y, g_ref[...], b_ref[...])


def _merge_call(ya, yb, p, h, mod, w_pa, w_pb, w_out, ln_g, ln_b, alpha, seq, tm=512, chunks=2):
    n, d = h.shape
    per_batch = seq // tm
    ga_blk = (p.shape[1] - 2 * d) // d
    const = lambda shape: pl.BlockSpec(shape, lambda i: (0, 0), pipeline_mode=pl.Buffered(1))
    return pl.pallas_call(
        functools.partial(_merge_kernel, alpha=alpha, chunks=chunks),
        grid=(n // tm,),
        in_specs=[
            pl.BlockSpec((tm, ya.shape[1]), lambda i: (i, 0)),
            pl.BlockSpec((tm, yb.shape[1]), lambda i: (i, 0)),
            pl.BlockSpec((tm, d), lambda i: (i, ga_blk)),
            pl.BlockSpec((tm, d), lambda i: (i, ga_blk + 1)),
            pl.BlockSpec((tm, d), lambda i: (i, 0)),
            pl.BlockSpec((1, N_MOD, d), lambda i: (i // per_batch, 0, 0)),
            const(w_pa.shape),
            const(w_pb.shape),
            const(w_out.shape),
            const((1, d)),
            const((1, d)),
        ],
        out_specs=pl.BlockSpec((tm, d), lambda i: (i, 0)),
        out_shape=jax.ShapeDtypeStruct((n, d), F32),
        compiler_params=_params("parallel"),
        name="merge",
    )(ya, yb, p, p, h, mod, w_pa, w_pb, w_out, ln_g, ln_b)


def _extended_rel_bias(rel_bias):
    heads, rel_size = rel_bias.shape
    n_low = (Q_SUB - 1) + (K_WIN - Q_SUB) - REL_CLIP
    n_high = E_LEN - n_low - rel_size
    e = jnp.concatenate([
        jnp.broadcast_to(rel_bias[:, :1], (heads, n_low)),
        rel_bias,
        jnp.broadcast_to(rel_bias[:, -1:], (heads, n_high)),
    ], axis=1)
    return e.reshape(heads, 1, E_LEN).astype(F32)


def kernel(x, c, w_ada, b_ada, ffn1_w_in, ffn1_w_out, ln1_g, ln1_b, w_mix_in, rel_bias, w_alpha2, b_alpha,
           gla_norm_g, w_proj_a, w_proj_b, w_mix_out, ln2_g, ln2_b, ffn2_w_in, ffn2_w_out, ln3_g, ln3_b):
    batch, seq, d = x.shape
    depth = w_ada.shape[0]
    alpha = (2.0 * depth) ** 0.25
    kdim = w_alpha2.shape[2]
    vdim = w_proj_b.shape[1]
    n_main = 3 * A_WIDTH + 2 * kdim + 2 * vdim
    row = lambda v: v.reshape(1, -1)

    h = x.reshape(batch * seq, d)
    c_pad = jnp.zeros((16, d), F32).at[:batch].set(c)
    for l in range(depth):
        mod = _mod_call(c_pad, w_ada[l], row(b_ada[l]))[:batch].reshape(batch, N_MOD, d)

        h = _ffn_call(h, mod, _stack_halves_call(ffn1_w_in[l]), ffn1_w_out[l].astype(BF16), row(ln1_g[l]), row(ln1_b[l]),
                      (0, 1, 2), alpha, seq)

        w_main_t, w_lr_t = _repack_call(jnp.swapaxes(w_mix_in, 1, 2)[l], n_main)
        p, lr = _mix_in_call(h, mod, w_main_t, w_lr_t, seq)

        ya, (w_pa, w_pb, w_mo, w2_in, w2_out) = _attn_call(
            p, _bias_table_call(_extended_rel_bias(rel_bias[l])), batch, seq,
            (w_proj_a[l], w_proj_b[l], w_mix_out[l], ffn2_w_in[l], ffn2_w_out[l]), (False, False, False, True, False))
        w_a2 = jnp.zeros((LANES, kdim), BF16).at[:GATE_RANK].set(w_alpha2[l].astype(BF16))
        yb = _gla_call(p, lr, w_a2, row(b_alpha[l]), row(gla_norm_g[l]), batch, seq, kdim, vdim)

        h = _merge_call(ya, yb, p, h, mod, w_pa, w_pb, w_mo, row(ln2_g[l]), row(ln2_b[l]), alpha, seq)

        h = _ffn_call(h, mod, w2_in, w2_out, row(ln3_g[l]), row(ln3_b[l]), (6, 7, 8), alpha, seq)
    return h.reshape(batch, seq, d)
```

```python
import functools

import jax
import jax.numpy as jnp
from jax import lax
from jax.experimental import pallas as pl
from jax.experimental.pallas import tpu as pltpu

F32 = jnp.float32
BF16 = jnp.bfloat16

CHUNK = 64
A_HEADS = 16
A_HEAD_DIM = 64
A_WIDTH = A_HEADS * A_HEAD_DIM
A_PAST_CHUNKS = 8
REL_CLIP = 256
B_HEADS = 4
GATE_RANK = 16
GATE_TAU = 16.0
N_MOD = 9
LN_EPS = 1e-5
RMS_EPS = 1e-6

LANES = 128
SUBLANES = 8
VMEM_BYTES = 64 * 1024 * 1024
VMEM_LIMIT_BYTES = 56 * 1024 * 1024
FFN_VMEM_LIMIT_BYTES = VMEM_BYTES - 2 * 1024 * 1024

FFN_RING = 2
FFN_OUT_CHUNKS = 4

Q_CHUNKS = 4
Q_SUB = Q_CHUNKS * CHUNK
K_WIN = (A_PAST_CHUNKS + Q_CHUNKS) * CHUNK
ATTN_UNROLL = 14
E_LEN = Q_SUB + K_WIN


def _params(*sem, vmem_limit_bytes=VMEM_LIMIT_BYTES):
    return pltpu.CompilerParams(dimension_semantics=sem, vmem_limit_bytes=vmem_limit_bytes)


def _silu(v):
    return v * jax.nn.sigmoid(v)


def _deepnorm(h, r, alpha, g, b):
    y = h + r
    mu = jnp.mean(y, axis=-1, keepdims=True)
    d = y - mu
    var = jnp.mean(d * d, axis=-1, keepdims=True)
    return d * lax.rsqrt(var + LN_EPS / (alpha * alpha)) * g + b


def _mod_kernel(c_ref, w_ref, b_ref, o_ref):
    s = _silu(c_ref[...]).astype(BF16)
    o_ref[...] = jnp.dot(s, w_ref[...].astype(BF16), preferred_element_type=F32) + b_ref[...]


def _mod_call(c_pad, w_ada, b_ada, tn=1024):
    rows, d = c_pad.shape
    n = w_ada.shape[1]
    return pl.pallas_call(
        _mod_kernel,
        grid=(n // tn,),
        in_specs=[
            pl.BlockSpec((rows, d), lambda j: (0, 0)),
            pl.BlockSpec((d, tn), lambda j: (0, j)),
            pl.BlockSpec((1, tn), lambda j: (0, j)),
        ],
        out_specs=pl.BlockSpec((rows, tn), lambda j: (0, j)),
        out_shape=jax.ShapeDtypeStruct((rows, n), F32),
        compiler_params=_params("parallel"),
        name="mod",
    )(c_pad, w_ada, b_ada)


def _ffn_kernel(h_ref, mod_ref, g_ref, b_ref, win_hbm, wout_hbm, o_ref, u_ref, wa_buf, wb_buf, wo_buf, sem,
                *, rows, alpha, out_chunks, n_tiles, nj):
    sh, sc, gt = rows
    tf = wo_buf.shape[1]
    base = pl.program_id(0) * nj
    total = n_tiles * nj

    def copies(s):
        j = lax.rem(s, nj)
        slot = lax.rem(s, FFN_RING)
        col_a = pl.multiple_of(j * tf, tf)
        col_b = pl.multiple_of((nj + j) * tf, tf)
        return (
            pltpu.make_async_copy(win_hbm.at[:, pl.ds(col_a, tf)], wa_buf.at[slot], sem.at[0, slot]),
            pltpu.make_async_copy(win_hbm.at[:, pl.ds(col_b, tf)], wb_buf.at[slot], sem.at[1, slot]),
            pltpu.make_async_copy(wout_hbm.at[pl.ds(col_a, tf), :], wo_buf.at[slot], sem.at[2, slot]),
        )

    def start(s):
        for c in copies(s):
            c.start()

    @pl.when(pl.program_id(0) == 0)
    def _():
        for s in range(FFN_RING - 1):
            start(s)

    def begin_step(s):
        @pl.when(s + (FFN_RING - 1) < total)
        def _():
            start(s + (FFN_RING - 1))

        for c in copies(s):
            c.wait()
        return lax.rem(s, FFN_RING)

    def gated(u, slot):
        a = jnp.dot(u, wa_buf[slot], preferred_element_type=F32)
        b = jnp.dot(u, wb_buf[slot], preferred_element_type=F32)
        return (_silu(a) * b).astype(BF16)

    slot = begin_step(base)
    u = (h_ref[...] * (1.0 + mod_ref[0, sc:sc + 1, :]) + mod_ref[0, sh:sh + 1, :]).astype(BF16)
    u_ref[...] = u
    o_ref[...] = jnp.dot(gated(u, slot), wo_buf[slot], preferred_element_type=F32)

    def middle(j, carry):
        slot = begin_step(base + j)
        o_ref[...] += jnp.dot(gated(u_ref[...], slot), wo_buf[slot], preferred_element_type=F32)
        return carry

    lax.fori_loop(1, nj - 1, middle, 0)

    slot = begin_step(base + (nj - 1))
    g = gated(u_ref[...], slot)
    tr = o_ref.shape[0] // out_chunks
    for r in range(out_chunks):
        rs = slice(r * tr, (r + 1) * tr)
        f = o_ref[rs, :] + jnp.dot(g[rs], wo_buf[slot], preferred_element_type=F32)
        o_ref[rs, :] = _deepnorm(h_ref[rs, :], ((0.5 / alpha) * mod_ref[0, gt:gt + 1, :]) * f, alpha,
                                 g_ref[...], b_ref[...])


def _ffn_call(h, mod, w_in, w_out, ln_g, ln_b, rows, alpha, seq, tm=1024, tf=512):
    n, d = h.shape
    dff = w_out.shape[0]
    nj = dff // tf
    per_batch = seq // tm
    return pl.pallas_call(
        functools.partial(_ffn_kernel, rows=rows, alpha=alpha, out_chunks=FFN_OUT_CHUNKS, n_tiles=n // tm, nj=nj),
        grid=(n // tm,),
        in_specs=[
            pl.BlockSpec((tm, d), lambda i: (i, 0)),
            pl.BlockSpec((1, N_MOD, d), lambda i: (i // per_batch, 0, 0)),
            pl.BlockSpec((1, d), lambda i: (0, 0)),
            pl.BlockSpec((1, d), lambda i: (0, 0)),
            pl.BlockSpec(memory_space=pl.ANY),
            pl.BlockSpec(memory_space=pl.ANY),
        ],
        out_specs=pl.BlockSpec((tm, d), lambda i: (i, 0)),
        out_shape=jax.ShapeDtypeStruct((n, d), F32),
        scratch_shapes=[
            pltpu.VMEM((tm, d), BF16),
            pltpu.VMEM((FFN_RING, d, tf), BF16),
            pltpu.VMEM((FFN_RING, d, tf), BF16),
            pltpu.VMEM((FFN_RING, tf, d), BF16),
            pltpu.SemaphoreType.DMA((3, FFN_RING)),
        ],
        compiler_params=_params("arbitrary", vmem_limit_bytes=FFN_VMEM_LIMIT_BYTES),
        name="ffn",
    )(h, mod, ln_g, ln_b, w_in, w_out)


def _repack_kernel(w_ref, wlr_ref, main_ref, lr_ref):
    main_ref[...] = w_ref[...].astype(BF16)
    row = lax.broadcasted_iota(jnp.int32, lr_ref.shape, 0)
    lr_ref[...] = jnp.where(row < GATE_RANK, wlr_ref[...], 0.0).astype(BF16)


def _repack_call(w_t, n_main, tr=1024):
    rows, d = w_t.shape
    main_blocks = n_main // tr
    src = lambda k: (SUBLANES * (k * (tr // SUBLANES) + jnp.where(k < main_blocks, 0, GATE_RANK // SUBLANES)), 0)
    return pl.pallas_call(
        _repack_kernel,
        grid=((rows - GATE_RANK) // tr,),
        in_specs=[
            pl.BlockSpec((pl.Element(tr), pl.Element(d)), src),
            pl.BlockSpec((pl.Element(LANES), pl.Element(d)), lambda k: (n_main, 0)),
        ],
        out_specs=[
            pl.BlockSpec((tr, d), lambda k: (k, 0)),
            pl.BlockSpec((LANES, d), lambda k: (0, 0)),
        ],
        out_shape=[
            jax.ShapeDtypeStruct((rows - GATE_RANK, d), BF16),
            jax.ShapeDtypeStruct((LANES, d), BF16),
        ],
        compiler_params=_params("arbitrary"),
        name="repack",
    )(w_t, w_t)


_NT = (((1,), (1,)), ((), ()))


def _mix_in_kernel(h_ref, mod_ref, w_ref, wlr_ref, p_ref, lr_ref, u_ref):
    j = pl.program_id(1)

    @pl.when(j == 0)
    def _():
        u = (h_ref[...] * (1.0 + mod_ref[0, 4:5, :]) + mod_ref[0, 3:4, :]).astype(BF16)
        u_ref[...] = u
        p_ref[...] = lax.dot_general(u, w_ref[...], _NT, preferred_element_type=F32).astype(BF16)
        lr_ref[...] = lax.dot_general(u, wlr_ref[...], _NT, preferred_element_type=F32)

    @pl.when(j > 0)
    def _():
        p_ref[...] = lax.dot_general(u_ref[...], w_ref[...], _NT, preferred_element_type=F32).astype(BF16)


def _mix_in_call(h, mod, w_main_t, w_lr_t, seq, tm=1024, tn=2048):
    n, d = h.shape
    cols = w_main_t.shape[0]
    per_batch = seq // tm
    return pl.pallas_call(
        _mix_in_kernel,
        grid=(n // tm, cols // tn),
        in_specs=[
            pl.BlockSpec((tm, d), lambda i, j: (i, 0)),
            pl.BlockSpec((1, N_MOD, d), lambda i, j: (i // per_batch, 0, 0)),
            pl.BlockSpec((tn, d), lambda i, j: (j, 0)),
            pl.BlockSpec((LANES, d), lambda i, j: (0, 0)),
        ],
        out_specs=[
            pl.BlockSpec((tm, tn), lambda i, j: (i, j)),
            pl.BlockSpec((tm, LANES), lambda i, j: (i, 0)),
        ],
        out_shape=[
            jax.ShapeDtypeStruct((n, cols), BF16),
            jax.ShapeDtypeStruct((n, LANES), F32),
        ],
        scratch_shapes=[pltpu.VMEM((tm, d), BF16)],
        compiler_params=_params("parallel", "arbitrary"),
        name="mix_in",
    )(h, mod, w_main_t, w_lr_t)


def _bias_table_kernel(e_ref, t_ref):
    row = lax.broadcasted_iota(jnp.int32, (Q_SUB, K_WIN), 0)
    col = lax.broadcasted_iota(jnp.int32, (Q_SUB, K_WIN), 1)
    q_chunk = row // CHUNK
    k_chunk = col // CHUNK
    in_band = (k_chunk >= q_chunk) & (k_chunk <= q_chunk + A_PAST_CHUNKS)
    for h in range(2):
        e = jnp.broadcast_to(e_ref[h], (Q_SUB, E_LEN))
        t = pltpu.roll(e, E_LEN - (Q_SUB - 1), 1, stride=1, stride_axis=0)[:, :K_WIN]
        t_ref[0, h * Q_SUB:(h + 1) * Q_SUB, :] = jnp.where(in_band, t, -jnp.inf)


def _bias_table_call(e_ext):
    pairs = e_ext.shape[0] // 2
    return pl.pallas_call(
        _bias_table_kernel,
        grid=(pairs,),
        in_specs=[pl.BlockSpec((2, 1, E_LEN), lambda hp: (hp, 0, 0))],
        out_specs=pl.BlockSpec((1, 2 * Q_SUB, K_WIN), lambda hp: (hp, 0, 0)),
        out_shape=jax.ShapeDtypeStruct((pairs, 2 * Q_SUB, K_WIN), F32),
        compiler_params=_params("parallel"),
        name="bias_table",
    )(e_ext)


def _attn_kernel(q_ref, k_ref, v_ref, t_ref, *refs, seq):
    n_cast = len(refs) // 2
    o_ref = refs[n_cast]
    for src_ref, dst_ref in zip(refs[:n_cast], refs[n_cast + 1:]):
        dst_ref[...] = src_ref[...].astype(dst_ref.dtype)

    lane = lax.broadcasted_iota(jnp.int32, (Q_SUB, 2 * A_HEAD_DIM), 1)
    head0 = lane < A_HEAD_DIM

    def sub_tile(q_start, k_start, width, t_start):
        q = q_ref[pl.ds(q_start, Q_SUB), :].astype(F32) * (A_HEAD_DIM ** -0.5)
        q2 = jnp.concatenate([jnp.where(head0, q, 0.0), jnp.where(head0, 0.0, q)], axis=0).astype(BF16)
        kw = k_ref[pl.ds(k_start, width), :]
        vw = v_ref[pl.ds(k_start, width), :]
        s = lax.dot_general(q2, kw, _NT, preferred_element_type=F32)
        s = s + t_ref[0, :, t_start:t_start + width]
        m = jnp.max(s, axis=-1, keepdims=True)
        p = jnp.exp(s - m)
        l = jnp.sum(p, axis=-1, keepdims=True)
        o = jnp.dot(p.astype(BF16), vw, preferred_element_type=F32) / l
        o_ref[pl.ds(q_start, Q_SUB), :] = jnp.where(head0, o[:Q_SUB], o[Q_SUB:]).astype(o_ref.dtype)

    n_head = (K_WIN - Q_SUB) // Q_SUB
    for j in range(n_head):
        sub_tile(j * Q_SUB, 0, (j + 1) * Q_SUB, K_WIN - (j + 1) * Q_SUB)

    def body(j, carry):
        q_start = pl.multiple_of(j * Q_SUB, Q_SUB)
        k_start = pl.multiple_of(j * Q_SUB - (K_WIN - Q_SUB), Q_SUB)
        sub_tile(q_start, k_start, K_WIN, 0)
        return carry

    lax.fori_loop(n_head, seq // Q_SUB, body, 0, unroll=ATTN_UNROLL)


def _attn_call(p, table, batch, seq, cast_weights):
    n = p.shape[0]
    pairs = A_HEADS // 2
    w = 2 * A_HEAD_DIM
    steps = batch * pairs
    slab = lambda a: pl.BlockSpec((a.shape[0] // steps, a.shape[1]), lambda b, hp: (b * pairs + hp, 0))
    outs = pl.pallas_call(
        functools.partial(_attn_kernel, seq=seq),
        grid=(batch, pairs),
        in_specs=[
            pl.BlockSpec((seq, w), lambda b, hp: (b, hp)),
            pl.BlockSpec((seq, w), lambda b, hp: (b, pairs + hp)),
            pl.BlockSpec((seq, w), lambda b, hp: (b, 2 * pairs + hp)),
            pl.BlockSpec((1, 2 * Q_SUB, K_WIN), lambda b, hp: (hp, 0, 0)),
        ] + [slab(a) for a in cast_weights],
        out_specs=[pl.BlockSpec((seq, w), lambda b, hp: (b, hp))] + [slab(a) for a in cast_weights],
        out_shape=[jax.ShapeDtypeStruct((n, A_WIDTH), BF16)]
        + [jax.ShapeDtypeStruct(a.shape, BF16) for a in cast_weights],
        compiler_params=_params("parallel", "parallel"),
        name="attn",
    )(p, p, p, table, *cast_weights)
    return outs[0], outs[1:]


def _gla_kernel(q_ref, k_ref, v_ref, r_ref, lr_ref, wa_ref, ba_ref, gn_ref, o_ref, st_ref, readout_ref,
                *, chunks, hk, hv):
    @pl.when(pl.program_id(1) == 0)
    def _():
        st_ref[...] = jnp.zeros_like(st_ref)

    rows = chunks * CHUNK
    z = jnp.dot(lr_ref[...].astype(BF16), wa_ref[...], preferred_element_type=F32) + ba_ref[...]
    log_a = -(jnp.maximum(-z, 0.0) + jnp.log(1.0 + jnp.exp(-jnp.abs(z)))) / GATE_TAU
    x0 = log_a.astype(BF16)
    x1 = (log_a - x0.astype(F32)).astype(BF16)
    ri = lax.broadcasted_iota(jnp.int32, (rows, rows), 0)
    ci = lax.broadcasted_iota(jnp.int32, (rows, rows), 1)
    later = jnp.logical_and(ri // CHUNK == ci // CHUNK, ci > ri).astype(BF16)
    rest = jnp.dot(later, x0, preferred_element_type=F32) + jnp.dot(later, x1, preferred_element_type=F32)
    rn = lax.broadcasted_iota(jnp.int32, (chunks, rows), 0)
    cn = lax.broadcasted_iota(jnp.int32, (chunks, rows), 1)
    member = (cn // CHUNK == rn).astype(BF16)
    total = jnp.dot(member, x0, preferred_element_type=F32) + jnp.dot(member, x1, preferred_element_type=F32)
    decay = jnp.exp(total)
    kdec = (k_ref[...].astype(F32) * jnp.exp(rest)).astype(BF16)

    for n in range(chunks):
        rs = slice(n * CHUNK, (n + 1) * CHUNK)
        for h in range(B_HEADS):
            ks = slice(h * hk, (h + 1) * hk)
            vs = slice(h * hv, (h + 1) * hv)
            upd = lax.dot_general(v_ref[rs, vs], kdec[rs, ks], (((0,), (0,)), ((), ())),
                                  preferred_element_type=F32)
            st = st_ref[h] * decay[n:n + 1, ks] + upd
            st_ref[h] = st
            readout_ref[rs, vs] = lax.dot_general(q_ref[rs, ks], st.astype(BF16), _NT, preferred_element_type=F32)

    for h in range(B_HEADS):
        vs = slice(h * hv, (h + 1) * hv)
        o = readout_ref[:, vs]
        o = o * lax.rsqrt(jnp.mean(o * o, axis=-1, keepdims=True) + RMS_EPS * hk) * gn_ref[...]
        o_ref[:, vs] = (o * _silu(r_ref[:, vs].astype(F32))).astype(o_ref.dtype)


def _gla_call(p, lr, w_alpha2, b_alpha, gn, batch, seq, kdim, vdim, rows=512):
    n = p.shape[0]
    steps = seq // rows
    hk, hv = kdim // B_HEADS, vdim // B_HEADS
    q_blk = (3 * A_WIDTH) // kdim
    v_blk = (3 * A_WIDTH + 2 * kdim) // vdim
    row_map = lambda col: (lambda b, t: (b * steps + t, col))
    return pl.pallas_call(
        functools.partial(_gla_kernel, chunks=rows // CHUNK, hk=hk, hv=hv),
        grid=(batch, steps),
        in_specs=[
            pl.BlockSpec((rows, kdim), row_map(q_blk)),
            pl.BlockSpec((rows, kdim), row_map(q_blk + 1)),
            pl.BlockSpec((rows, vdim), row_map(v_blk)),
            pl.BlockSpec((rows, vdim), row_map(v_blk + 1)),
            pl.BlockSpec((rows, LANES), row_map(0)),
            pl.BlockSpec((LANES, kdim), lambda b, t: (0, 0)),
            pl.BlockSpec((1, kdim), lambda b, t: (0, 0)),
            pl.BlockSpec((1, hv), lambda b, t: (0, 0)),
        ],
        out_specs=pl.BlockSpec((rows, vdim), row_map(0)),
        out_shape=jax.ShapeDtypeStruct((n, vdim), BF16),
        scratch_shapes=[pltpu.VMEM((B_HEADS, hv, hk), F32), pltpu.VMEM((rows, vdim), F32)],
        compiler_params=_params("parallel", "arbitrary"),
        name="gla",
    )(p, p, p, p, lr, w_alpha2, b_alpha, gn)


def _merge_kernel(ya_ref, yb_ref, ga_ref, gb_ref, h_ref, mod_ref, wpa_ref, wpb_ref, wo_ref, g_ref, b_ref, o_ref,
                  *, alpha, chunks):
    tr = o_ref.shape[0] // chunks
    for r in range(chunks):
        rs = slice(r * tr, (r + 1) * tr)
        pa = jnp.dot(ya_ref[rs, :], wpa_ref[...], preferred_element_type=F32)
        pb = jnp.dot(yb_ref[rs, :], wpb_ref[...], preferred_element_type=F32)
        merged = (jax.nn.sigmoid(ga_ref[rs, :].astype(F32)) * pa
                  + jax.nn.sigmoid(gb_ref[rs, :].astype(F32)) * pb)
        m = jnp.dot(merged.astype(BF16), wo_ref[...], preferred_element_type=F32)
        o_ref[rs, :] = _deepnorm(h_ref[rs, :], (mod_ref[0, 5:6, :] / alpha) * m, alpha, g_ref[...], b_ref[...])


def _merge_call(ya, yb, p, h, mod, w_pa, w_pb, w_out, ln_g, ln_b, alpha, seq, tm=512, chunks=2):
    n, d = h.shape
    per_batch = seq // tm
    ga_blk = (p.shape[1] - 2 * d) // d
    const = lambda shape: pl.BlockSpec(shape, lambda i: (0, 0), pipeline_mode=pl.Buffered(1))
    return pl.pallas_call(
        functools.partial(_merge_kernel, alpha=alpha, chunks=chunks),
        grid=(n // tm,),
        in_specs=[
            pl.BlockSpec((tm, ya.shape[1]), lambda i: (i, 0)),
            pl.BlockSpec((tm, yb.shape[1]), lambda i: (i, 0)),
            pl.BlockSpec((tm, d), lambda i: (i, ga_blk)),
            pl.BlockSpec((tm, d), lambda i: (i, ga_blk + 1)),
            pl.BlockSpec((tm, d), lambda i: (i, 0)),
            pl.BlockSpec((1, N_MOD, d), lambda i: (i // per_batch, 0, 0)),
            const(w_pa.shape),
            const(w_pb.shape),
            const(w_out.shape),
            const((1, d)),
            const((1, d)),
        ],
        out_specs=pl.BlockSpec((tm, d), lambda i: (i, 0)),
        out_shape=jax.ShapeDtypeStruct((n, d), F32),
        compiler_params=_params("parallel"),
        name="merge",
    )(ya, yb, p, p, h, mod, w_pa, w_pb, w_out, ln_g, ln_b)


def _extended_rel_bias(rel_bias):
    heads, rel_size = rel_bias.shape
    n_low = (Q_SUB - 1) + (K_WIN - Q_SUB) - REL_CLIP
    n_high = E_LEN - n_low - rel_size
    e = jnp.concatenate([
        jnp.broadcast_to(rel_bias[:, :1], (heads, n_low)),
        rel_bias,
        jnp.broadcast_to(rel_bias[:, -1:], (heads, n_high)),
    ], axis=1)
    return e.reshape(heads, 1, E_LEN).astype(F32)


def kernel(x, c, w_ada, b_ada, ffn1_w_in, ffn1_w_out, ln1_g, ln1_b, w_mix_in, rel_bias, w_alpha2, b_alpha,
           gla_norm_g, w_proj_a, w_proj_b, w_mix_out, ln2_g, ln2_b, ffn2_w_in, ffn2_w_out, ln3_g, ln3_b):
    batch, seq, d = x.shape
    depth = w_ada.shape[0]
    alpha = (2.0 * depth) ** 0.25
    kdim = w_alpha2.shape[2]
    vdim = w_proj_b.shape[1]
    n_main = 3 * A_WIDTH + 2 * kdim + 2 * vdim
    row = lambda v: v.reshape(1, -1)

    h = x.reshape(batch * seq, d)
    c_pad = jnp.zeros((16, d), F32).at[:batch].set(c)
    for l in range(depth):
        mod = _mod_call(c_pad, w_ada[l], row(b_ada[l]))[:batch].reshape(batch, N_MOD, d)

        h = _ffn_call(h, mod, ffn1_w_in[l].astype(BF16), ffn1_w_out[l].astype(BF16), row(ln1_g[l]), row(ln1_b[l]),
                      (0, 1, 2), alpha, seq)

        w_main_t, w_lr_t = _repack_call(jnp.swapaxes(w_mix_in, 1, 2)[l], n_main)
        p, lr = _mix_in_call(h, mod, w_main_t, w_lr_t, seq)

        ya, (w_pa, w_pb, w_mo, w2_in, w2_out) = _attn_call(
            p, _bias_table_call(_extended_rel_bias(rel_bias[l])), batch, seq,
            (w_proj_a[l], w_proj_b[l], w_mix_out[l], ffn2_w_in[l], ffn2_w_out[l]))
        w_a2 = jnp.zeros((LANES, kdim), BF16).at[:GATE_RANK].set(w_alpha2[l].astype(BF16))
        yb = _gla_call(p, lr, w_a2, row(b_alpha[l]), row(gla_norm_g[l]), batch, seq, kdim, vdim)

        h = _merge_call(ya, yb, p, h, mod, w_pa, w_pb, w_mo, row(ln2_g[l]), row(ln2_b[l]), alpha, seq)

        h = _ffn_call(h, mod, w2_in, w2_out, row(ln3_g[l]), row(ln3_b[l]), (6, 7, 8), alpha, seq)
    return h.reshape(batch, seq, d)
```

```python
import functools

import jax
import jax.numpy as jnp
from jax import lax
from jax.experimental import pallas as pl
from jax.experimental.pallas import tpu as pltpu

F32 = jnp.float32
BF16 = jnp.bfloat16

CHUNK = 64
A_HEADS = 16
A_HEAD_DIM = 64
A_WIDTH = A_HEADS * A_HEAD_DIM
A_PAST_CHUNKS = 8
REL_CLIP = 256
B_HEADS = 4
GATE_RANK = 16
GATE_TAU = 16.0
N_MOD = 9
LN_EPS = 1e-5
RMS_EPS = 1e-6

LANES = 128
SUBLANES = 8
BF16_SUBLANES = 16
VMEM_BYTES = 64 * 1024 * 1024
VMEM_LIMIT_BYTES = 56 * 1024 * 1024
FFN_VMEM_LIMIT_BYTES = VMEM_BYTES - 2 * 1024 * 1024

FFN_RING = 2
FFN_OUT_CHUNKS = 4

Q_CHUNKS = 4
Q_SUB = Q_CHUNKS * CHUNK
K_WIN = (A_PAST_CHUNKS + Q_CHUNKS) * CHUNK
ATTN_UNROLL = 14
E_LEN = Q_SUB + K_WIN


def _params(*sem, vmem_limit_bytes=VMEM_LIMIT_BYTES):
    return pltpu.CompilerParams(dimension_semantics=sem, vmem_limit_bytes=vmem_limit_bytes)


def _silu(v):
    return v / (1.0 + jnp.exp(-v))


def _deepnorm(h, r, alpha, g, b):
    y = h + r
    mu = jnp.mean(y, axis=-1, keepdims=True)
    d = y - mu
    var = jnp.mean(d * d, axis=-1, keepdims=True)
    return d * lax.rsqrt(var + LN_EPS / (alpha * alpha)) * g + b


def _mod_kernel(c_ref, w_ref, b_ref, o_ref):
    s = _silu(c_ref[...]).astype(BF16)
    o_ref[...] = jnp.dot(s, w_ref[...].astype(BF16), preferred_element_type=F32) + b_ref[...]


def _mod_call(c_pad, w_ada, b_ada, tn=1024):
    rows, d = c_pad.shape
    n = w_ada.shape[1]
    return pl.pallas_call(
        _mod_kernel,
        grid=(n // tn,),
        in_specs=[
            pl.BlockSpec((rows, d), lambda j: (0, 0)),
            pl.BlockSpec((d, tn), lambda j: (0, j)),
            pl.BlockSpec((1, tn), lambda j: (0, j)),
        ],
        out_specs=pl.BlockSpec((rows, tn), lambda j: (0, j)),
        out_shape=jax.ShapeDtypeStruct((rows, n), F32),
        compiler_params=_params("parallel"),
        name="mod",
    )(c_pad, w_ada, b_ada)


def _ffn_kernel(h_ref, mod_ref, g_ref, b_ref, win_hbm, wout_hbm, o_ref, u_ref, wa_buf, wb_buf, wo_buf, sem,
                *, rows, alpha, out_chunks, n_tiles, nj):
    sh, sc, gt = rows
    tf = wo_buf.shape[1]
    base = pl.program_id(0) * nj
    total = n_tiles * nj

    def copies(s):
        j = lax.rem(s, nj)
        slot = lax.rem(s, FFN_RING)
        col_a = pl.multiple_of(j * tf, tf)
        col_b = pl.multiple_of((nj + j) * tf, tf)
        return (
            pltpu.make_async_copy(win_hbm.at[:, pl.ds(col_a, tf)], wa_buf.at[slot], sem.at[0, slot]),
            pltpu.make_async_copy(win_hbm.at[:, pl.ds(col_b, tf)], wb_buf.at[slot], sem.at[1, slot]),
            pltpu.make_async_copy(wout_hbm.at[pl.ds(col_a, tf), :], wo_buf.at[slot], sem.at[2, slot]),
        )

    def start(s):
        for c in copies(s):
            c.start()

    @pl.when(pl.program_id(0) == 0)
    def _():
        for s in range(FFN_RING - 1):
            start(s)

    def begin_step(s):
        @pl.when(s + (FFN_RING - 1) < total)
        def _():
            start(s + (FFN_RING - 1))

        for c in copies(s):
            c.wait()
        return lax.rem(s, FFN_RING)

    def gated(u, slot):
        a = jnp.dot(u, wa_buf[slot], preferred_element_type=F32)
        b = jnp.dot(u, wb_buf[slot], preferred_element_type=F32)
        return (_silu(a) * b).astype(BF16)

    slot = begin_step(base)
    u = (h_ref[...] * (1.0 + mod_ref[0, sc:sc + 1, :]) + mod_ref[0, sh:sh + 1, :]).astype(BF16)
    u_ref[...] = u
    o_ref[...] = jnp.dot(gated(u, slot), wo_buf[slot], preferred_element_type=F32)

    def middle(j, carry):
        slot = begin_step(base + j)
        o_ref[...] += jnp.dot(gated(u_ref[...], slot), wo_buf[slot], preferred_element_type=F32)
        return carry

    lax.fori_loop(1, nj - 1, middle, 0)

    slot = begin_step(base + (nj - 1))
    g = gated(u_ref[...], slot)
    tr = o_ref.shape[0] // out_chunks
    for r in range(out_chunks):
        rs = slice(r * tr, (r + 1) * tr)
        f = o_ref[rs, :] + jnp.dot(g[rs], wo_buf[slot], preferred_element_type=F32)
        o_ref[rs, :] = _deepnorm(h_ref[rs, :], ((0.5 / alpha) * mod_ref[0, gt:gt + 1, :]) * f, alpha,
                                 g_ref[...], b_ref[...])


def _ffn_call(h, mod, w_in, w_out, ln_g, ln_b, rows, alpha, seq, tm=1024, tf=512):
    n, d = h.shape
    dff = w_out.shape[0]
    nj = dff // tf
    per_batch = seq // tm
    return pl.pallas_call(
        functools.partial(_ffn_kernel, rows=rows, alpha=alpha, out_chunks=FFN_OUT_CHUNKS, n_tiles=n // tm, nj=nj),
        grid=(n // tm,),
        in_specs=[
            pl.BlockSpec((tm, d), lambda i: (i, 0)),
            pl.BlockSpec((1, N_MOD, d), lambda i: (i // per_batch, 0, 0)),
            pl.BlockSpec((1, d), lambda i: (0, 0)),
            pl.BlockSpec((1, d), lambda i: (0, 0)),
            pl.BlockSpec(memory_space=pl.ANY),
            pl.BlockSpec(memory_space=pl.ANY),
        ],
        out_specs=pl.BlockSpec((tm, d), lambda i: (i, 0)),
        out_shape=jax.ShapeDtypeStruct((n, d), F32),
        scratch_shapes=[
            pltpu.VMEM((tm, d), BF16),
            pltpu.VMEM((FFN_RING, d, tf), BF16),
            pltpu.VMEM((FFN_RING, d, tf), BF16),
            pltpu.VMEM((FFN_RING, tf, d), BF16),
            pltpu.SemaphoreType.DMA((3, FFN_RING)),
        ],
        compiler_params=_params("arbitrary", vmem_limit_bytes=FFN_VMEM_LIMIT_BYTES),
        name="ffn",
    )(h, mod, ln_g, ln_b, w_in, w_out)


def _repack_kernel(w_ref, wlr_ref, main_ref, lr_ref):
    main_ref[...] = w_ref[...].astype(BF16)
    row = lax.broadcasted_iota(jnp.int32, lr_ref.shape, 0)
    lr_ref[...] = jnp.where(row < GATE_RANK, wlr_ref[...], 0.0).astype(BF16)


def _repack_call(w_t, n_main, tr=1024):
    rows, d = w_t.shape
    main_blocks = n_main // tr
    src = lambda k: (SUBLANES * (k * (tr // SUBLANES) + jnp.where(k < main_blocks, 0, GATE_RANK // SUBLANES)), 0)
    return pl.pallas_call(
        _repack_kernel,
        grid=((rows - GATE_RANK) // tr,),
        in_specs=[
            pl.BlockSpec((pl.Element(tr), pl.Element(d)), src),
            pl.BlockSpec((pl.Element(LANES), pl.Element(d)), lambda k: (n_main, 0)),
        ],
        out_specs=[
            pl.BlockSpec((tr, d), lambda k: (k, 0)),
            pl.BlockSpec((LANES, d), lambda k: (0, 0)),
        ],
        out_shape=[
            jax.ShapeDtypeStruct((rows - GATE_RANK, d), BF16),
            jax.ShapeDtypeStruct((LANES, d), BF16),
        ],
        compiler_params=_params("arbitrary"),
        name="repack",
    )(w_t, w_t)


_NT = (((1,), (1,)), ((), ()))


def _mix_in_kernel(h_ref, mod_ref, w_ref, wlr_ref, p_ref, lr_ref, u_ref):
    j = pl.program_id(1)

    @pl.when(j == 0)
    def _():
        u = (h_ref[...] * (1.0 + mod_ref[0, 4:5, :]) + mod_ref[0, 3:4, :]).astype(BF16)
        u_ref[...] = u
        p_ref[...] = lax.dot_general(u, w_ref[...], _NT, preferred_element_type=F32).astype(BF16)
        lr_ref[...] = lax.dot_general(u, wlr_ref[...], _NT, preferred_element_type=F32)

    @pl.when(j > 0)
    def _():
        p_ref[...] = lax.dot_general(u_ref[...], w_ref[...], _NT, preferred_element_type=F32).astype(BF16)


def _mix_in_call(h, mod, w_main_t, w_lr_t, seq, tm=1024, tn=2048):
    n, d = h.shape
    cols = w_main_t.shape[0]
    per_batch = seq // tm
    return pl.pallas_call(
        _mix_in_kernel,
        grid=(n // tm, cols // tn),
        in_specs=[
            pl.BlockSpec((tm, d), lambda i, j: (i, 0)),
            pl.BlockSpec((1, N_MOD, d), lambda i, j: (i // per_batch, 0, 0)),
            pl.BlockSpec((tn, d), lambda i, j: (j, 0)),
            pl.BlockSpec((LANES, d), lambda i, j: (0, 0)),
        ],
        out_specs=[
            pl.BlockSpec((tm, tn), lambda i, j: (i, j)),
            pl.BlockSpec((tm, LANES), lambda i, j: (i, 0)),
        ],
        out_shape=[
            jax.ShapeDtypeStruct((n, cols), BF16),
            jax.ShapeDtypeStruct((n, LANES), F32),
        ],
        scratch_shapes=[pltpu.VMEM((tm, d), BF16)],
        compiler_params=_params("parallel", "arbitrary"),
        name="mix_in",
    )(h, mod, w_main_t, w_lr_t)


def _bias_table_kernel(e_ref, t_ref):
    row = lax.broadcasted_iota(jnp.int32, (Q_SUB, K_WIN), 0)
    col = lax.broadcasted_iota(jnp.int32, (Q_SUB, K_WIN), 1)
    q_chunk = row // CHUNK
    k_chunk = col // CHUNK
    in_band = (k_chunk >= q_chunk) & (k_chunk <= q_chunk + A_PAST_CHUNKS)
    for h in range(2):
        e = jnp.broadcast_to(e_ref[h], (Q_SUB, E_LEN))
        t = pltpu.roll(e, E_LEN - (Q_SUB - 1), 1, stride=1, stride_axis=0)[:, :K_WIN]
        t_ref[0, h * Q_SUB:(h + 1) * Q_SUB, :] = jnp.where(in_band, t, -jnp.inf)


def _bias_table_call(e_ext):
    pairs = e_ext.shape[0] // 2
    return pl.pallas_call(
        _bias_table_kernel,
        grid=(pairs,),
        in_specs=[pl.BlockSpec((2, 1, E_LEN), lambda hp: (hp, 0, 0))],
        out_specs=pl.BlockSpec((1, 2 * Q_SUB, K_WIN), lambda hp: (hp, 0, 0)),
        out_shape=jax.ShapeDtypeStruct((pairs, 2 * Q_SUB, K_WIN), F32),
        compiler_params=_params("parallel"),
        name="bias_table",
    )(e_ext)


def _attn_kernel(q_ref, k_ref, v_ref, t_ref, *refs, seq):
    n_cast = len(refs) // 2
    o_ref = refs[n_cast]
    for src_ref, dst_ref in zip(refs[:n_cast], refs[n_cast + 1:]):
        dst_ref[...] = src_ref[...].astype(dst_ref.dtype)

    lane = lax.broadcasted_iota(jnp.int32, (Q_SUB, 2 * A_HEAD_DIM), 1)
    head0 = lane < A_HEAD_DIM

    def sub_tile(q_start, k_start, width, t_start):
        q = q_ref[pl.ds(q_start, Q_SUB), :].astype(F32) * (A_HEAD_DIM ** -0.5)
        q2 = jnp.concatenate([jnp.where(head0, q, 0.0), jnp.where(head0, 0.0, q)], axis=0).astype(BF16)
        kw = k_ref[pl.ds(k_start, width), :]
        vw = v_ref[pl.ds(k_start, width), :]
        s = lax.dot_general(q2, kw, _NT, preferred_element_type=F32)
        s = s + t_ref[0, :, t_start:t_start + width]
        m = jnp.max(s, axis=-1, keepdims=True)
        p = jnp.exp(s - m)
        l = jnp.sum(p, axis=-1, keepdims=True)
        o = jnp.dot(p.astype(BF16), vw, preferred_element_type=F32) / l
        o_ref[pl.ds(q_start, Q_SUB), :] = jnp.where(head0, o[:Q_SUB], o[Q_SUB:]).astype(o_ref.dtype)

    n_head = (K_WIN - Q_SUB) // Q_SUB
    for j in range(n_head):
        sub_tile(j * Q_SUB, 0, (j + 1) * Q_SUB, K_WIN - (j + 1) * Q_SUB)

    def body(j, carry):
        q_start = pl.multiple_of(j * Q_SUB, Q_SUB)
        k_start = pl.multiple_of(j * Q_SUB - (K_WIN - Q_SUB), Q_SUB)
        sub_tile(q_start, k_start, K_WIN, 0)
        return carry

    lax.fori_loop(n_head, seq // Q_SUB, body, 0, unroll=ATTN_UNROLL)


def _attn_call(p, table, batch, seq, cast_weights):
    n = p.shape[0]
    pairs = A_HEADS // 2
    w = 2 * A_HEAD_DIM
    steps = batch * pairs
    slab = lambda a: pl.BlockSpec((a.shape[0] // steps, a.shape[1]), lambda b, hp: (b * pairs + hp, 0))
    outs = pl.pallas_call(
        functools.partial(_attn_kernel, seq=seq),
        grid=(batch, pairs),
        in_specs=[
            pl.BlockSpec((seq, w), lambda b, hp: (b, hp)),
            pl.BlockSpec((seq, w), lambda b, hp: (b, pairs + hp)),
            pl.BlockSpec((seq, w), lambda b, hp: (b, 2 * pairs + hp)),
            pl.BlockSpec((1, 2 * Q_SUB, K_WIN), lambda b, hp: (hp, 0, 0)),
        ] + [slab(a) for a in cast_weights],
        out_specs=[pl.BlockSpec((seq, w), lambda b, hp: (b, hp))] + [slab(a) for a in cast_weights],
        out_shape=[jax.ShapeDtypeStruct((n, A_WIDTH), BF16)]
        + [jax.ShapeDtypeStruct(a.shape, BF16) for a in cast_weights],
        compiler_params=_params("parallel", "parallel"),
        name="attn",
    )(p, p, p, table, *cast_weights)
    return outs[0], outs[1:]


def _gla_kernel(q_ref, k_ref, v_ref, r_ref, lr_ref, wa_ref, ba_ref, gn_ref, later_ref, member_ref, o_ref, st_ref,
                readout_ref,
                *, chunks, hk, hv):
    @pl.when(pl.program_id(1) == 0)
    def _():
        st_ref[...] = jnp.zeros_like(st_ref)

    rows = chunks * CHUNK
    z = jnp.dot(lr_ref[...].astype(BF16), wa_ref[...], preferred_element_type=F32) + ba_ref[...]
    log_a = -(jnp.maximum(-z, 0.0) + jnp.log(1.0 + jnp.exp(-jnp.abs(z)))) / GATE_TAU
    x0 = log_a.astype(BF16)
    x1 = (log_a - x0.astype(F32)).astype(BF16)
    later = later_ref[...]
    member = member_ref[...]
    rest = jnp.dot(later, x0, preferred_element_type=F32) + jnp.dot(later, x1, preferred_element_type=F32)
    total = jnp.dot(member, x0, preferred_element_type=F32) + jnp.dot(member, x1, preferred_element_type=F32)
    decay = jnp.exp(total)
    kdec = (k_ref[...].astype(F32) * jnp.exp(rest)).astype(BF16)

    for n in range(chunks):
        rs = slice(n * CHUNK, (n + 1) * CHUNK)
        for h in range(B_HEADS):
            ks = slice(h * hk, (h + 1) * hk)
            vs = slice(h * hv, (h + 1) * hv)
            upd = lax.dot_general(v_ref[rs, vs], kdec[rs, ks], (((0,), (0,)), ((), ())),
                                  preferred_element_type=F32)
            st = st_ref[h] * decay[n:n + 1, ks] + upd
            st_ref[h] = st
            readout_ref[rs, vs] = lax.dot_general(q_ref[rs, ks], st.astype(BF16), _NT, preferred_element_type=F32)

    for h in range(B_HEADS):
        vs = slice(h * hv, (h + 1) * hv)
        o = readout_ref[:, vs]
        o = o * lax.rsqrt(jnp.mean(o * o, axis=-1, keepdims=True) + RMS_EPS * hk) * gn_ref[...]
        o_ref[:, vs] = (o * _silu(r_ref[:, vs].astype(F32))).astype(o_ref.dtype)


def _gla_call(p, lr, w_alpha2, b_alpha, gn, batch, seq, kdim, vdim, rows=512):
    n = p.shape[0]
    steps = seq // rows
    hk, hv = kdim // B_HEADS, vdim // B_HEADS
    q_blk = (3 * A_WIDTH) // kdim
    v_blk = (3 * A_WIDTH + 2 * kdim) // vdim
    row_map = lambda col: (lambda b, t: (b * steps + t, col))
    frame = jnp.arange(rows)
    same_chunk = frame[:, None] // CHUNK == frame[None, :] // CHUNK
    later = jnp.logical_and(same_chunk, frame[None, :] > frame[:, None]).astype(BF16)
    member = (frame[None, :] // CHUNK == jnp.arange(rows // CHUNK)[:, None]).astype(BF16)
    return pl.pallas_call(
        functools.partial(_gla_kernel, chunks=rows // CHUNK, hk=hk, hv=hv),
        grid=(batch, steps),
        in_specs=[
            pl.BlockSpec((rows, kdim), row_map(q_blk)),
            pl.BlockSpec((rows, kdim), row_map(q_blk + 1)),
            pl.BlockSpec((rows, vdim), row_map(v_blk)),
            pl.BlockSpec((rows, vdim), row_map(v_blk + 1)),
            pl.BlockSpec((rows, LANES), row_map(0)),
            pl.BlockSpec((LANES, kdim), lambda b, t: (0, 0)),
            pl.BlockSpec((1, kdim), lambda b, t: (0, 0)),
            pl.BlockSpec((1, hv), lambda b, t: (0, 0)),
            pl.BlockSpec((rows, rows), lambda b, t: (0, 0)),
            pl.BlockSpec((rows // CHUNK, rows), lambda b, t: (0, 0)),
        ],
        out_specs=pl.BlockSpec((rows, vdim), row_map(0)),
        out_shape=jax.ShapeDtypeStruct((n, vdim), BF16),
        scratch_shapes=[pltpu.VMEM((B_HEADS, hv, hk), F32), pltpu.VMEM((rows, vdim), F32)],
        compiler_params=_params("parallel", "arbitrary"),
        name="gla",
    )(p, p, p, p, lr, w_alpha2, b_alpha, gn, later, member)


def _merge_kernel(ya_ref, yb_ref, ga_ref, gb_ref, h_ref, mod_ref, wpa_ref, wpb_ref, wo_ref, g_ref, b_ref, o_ref,
                  *, alpha, chunks):
    tr = o_ref.shape[0] // chunks
    for r in range(chunks):
        rs = slice(r * tr, (r + 1) * tr)
        pa = jnp.dot(ya_ref[rs, :], wpa_ref[...], preferred_element_type=F32)
        pb = jnp.dot(yb_ref[rs, :], wpb_ref[...], preferred_element_type=F32)
        merged = (jax.nn.sigmoid(ga_ref[rs, :].astype(F32)) * pa
                  + jax.nn.sigmoid(gb_ref[rs, :].astype(F32)) * pb)
        m = jnp.dot(merged.astype(BF16), wo_ref[...], preferred_element_type=F32)
        o_ref[rs, :] = _deepnorm(h_ref[rs, :], (mod_ref[0, 5:6, :] / alpha) * m, alpha, g_ref[...], b_ref[...])


def _merge_call(ya, yb, p, h, mod, w_pa, w_pb, w_out, ln_g, ln_b, alpha, seq, tm=512, chunks=2):
    n, d = h.shape
    per_batch = seq // tm
    ga_blk = (p.shape[1] - 2 * d) // d
    const = lambda shape: pl.BlockSpec(shape, lambda i: (0, 0), pipeline_mode=pl.Buffered(1))
    return pl.pallas_call(
        functools.partial(_merge_kernel, alpha=alpha, chunks=chunks),
        grid=(n // tm,),
        in_specs=[
            pl.BlockSpec((tm, ya.shape[1]), lambda i: (i, 0)),
            pl.BlockSpec((tm, yb.shape[1]), lambda i: (i, 0)),
            pl.BlockSpec((tm, d), lambda i: (i, ga_blk)),
            pl.BlockSpec((tm, d), lambda i: (i, ga_blk + 1)),
            pl.BlockSpec((tm, d), lambda i: (i, 0)),
            pl.BlockSpec((1, N_MOD, d), lambda i: (i // per_batch, 0, 0)),
            const(w_pa.shape),
            const(w_pb.shape),
            const(w_out.shape),
            const((1, d)),
            const((1, d)),
        ],
        out_specs=pl.BlockSpec((tm, d), lambda i: (i, 0)),
        out_shape=jax.ShapeDtypeStruct((n, d), F32),
        compiler_params=_params("parallel"),
        name="merge",
    )(ya, yb, p, p, h, mod, w_pa, w_pb, w_out, ln_g, ln_b)


def _extended_rel_bias(rel_bias):
    heads, rel_size = rel_bias.shape
    n_low = (Q_SUB - 1) + (K_WIN - Q_SUB) - REL_CLIP
    n_high = E_LEN - n_low - rel_size
    e = jnp.concatenate([
        jnp.broadcast_to(rel_bias[:, :1], (heads, n_low)),
        rel_bias,
        jnp.broadcast_to(rel_bias[:, -1:], (heads, n_high)),
    ], axis=1)
    return e.reshape(heads, 1, E_LEN).astype(F32)


def kernel(x, c, w_ada, b_ada, ffn1_w_in, ffn1_w_out, ln1_g, ln1_b, w_mix_in, rel_bias, w_alpha2, b_alpha,
           gla_norm_g, w_proj_a, w_proj_b, w_mix_out, ln2_g, ln2_b, ffn2_w_in, ffn2_w_out, ln3_g, ln3_b):
    batch, seq, d = x.shape
    depth = w_ada.shape[0]
    alpha = (2.0 * depth) ** 0.25
    kdim = w_alpha2.shape[2]
    vdim = w_proj_b.shape[1]
    n_main = 3 * A_WIDTH + 2 * kdim + 2 * vdim
    row = lambda v: v.reshape(1, -1)

    h = x.reshape(batch * seq, d)
    c_pad = jnp.zeros((BF16_SUBLANES, d), F32).at[:batch].set(c)
    for l in range(depth):
        mod = _mod_call(c_pad, w_ada[l], row(b_ada[l]))[:batch].reshape(batch, N_MOD, d)

        h = _ffn_call(h, mod, ffn1_w_in[l].astype(BF16), ffn1_w_out[l].astype(BF16), row(ln1_g[l]), row(ln1_b[l]),
                      (0, 1, 2), alpha, seq)

        w_main_t, w_lr_t = _repack_call(jnp.swapaxes(w_mix_in, 1, 2)[l], n_main)
        p, lr = _mix_in_call(h, mod, w_main_t, w_lr_t, seq)

        ya, (w_pa, w_pb, w_mo, w2_in, w2_out) = _attn_call(
            p, _bias_table_call(_extended_rel_bias(rel_bias[l])), batch, seq,
            (w_proj_a[l], w_proj_b[l], w_mix_out[l], ffn2_w_in[l], ffn2_w_out[l]))
        w_a2 = jnp.zeros((LANES, kdim), BF16).at[:GATE_RANK].set(w_alpha2[l].astype(BF16))
        yb = _gla_call(p, lr, w_a2, row(b_alpha[l]), row(gla_norm_g[l]), batch, seq, kdim, vdim)

        h = _merge_call(ya, yb, p, h, mod, w_pa, w_pb, w_mo, row(ln2_g[l]), row(ln2_b[l]), alpha, seq)

        h = _ffn_call(h, mod, w2_in, w2_out, row(ln3_g[l]), row(ln3_b[l]), (6, 7, 8), alpha, seq)
    return h.reshape(batch, seq, d)
```

```python
import functools

import jax
import jax.numpy as jnp
from jax import lax
from jax.experimental import pallas as pl
from jax.experimental.pallas import tpu as pltpu

F32 = jnp.float32
BF16 = jnp.bfloat16

CHUNK = 64
A_HEADS = 16
A_HEAD_DIM = 64
A_WIDTH = A_HEADS * A_HEAD_DIM
A_PAST_CHUNKS = 8
REL_CLIP = 256
B_HEADS = 4
GATE_RANK = 16
GATE_TAU = 16.0
N_MOD = 9
LN_EPS = 1e-5
RMS_EPS = 1e-6

LANES = 128
SUBLANES = 8
BF16_SUBLANES = 16
VMEM_BYTES = 64 * 1024 * 1024
VMEM_LIMIT_BYTES = 56 * 1024 * 1024
FFN_VMEM_LIMIT_BYTES = VMEM_BYTES - 2 * 1024 * 1024

FFN_RING = 2
FFN_OUT_CHUNKS = 4

Q_CHUNKS = 4
Q_SUB = Q_CHUNKS * CHUNK
K_WIN = (A_PAST_CHUNKS + Q_CHUNKS) * CHUNK
ATTN_UNROLL = 14
E_LEN = Q_SUB + K_WIN


def _params(*sem, vmem_limit_bytes=VMEM_LIMIT_BYTES):
    return pltpu.CompilerParams(dimension_semantics=sem, vmem_limit_bytes=vmem_limit_bytes)


def _silu(v):
    return v / (1.0 + jnp.exp(-v))


def _deepnorm(h, r, alpha, g, b):
    y = h + r
    mu = jnp.mean(y, axis=-1, keepdims=True)
    d = y - mu
    var = jnp.mean(d * d, axis=-1, keepdims=True)
    return d * lax.rsqrt(var + LN_EPS / (alpha * alpha)) * g + b


def _mod_kernel(c_ref, w_ref, b_ref, o_ref):
    s = _silu(c_ref[...]).astype(BF16)
    o_ref[...] = jnp.dot(s, w_ref[...].astype(BF16), preferred_element_type=F32) + b_ref[...]


def _mod_call(c_pad, w_ada, b_ada, tn=1024):
    rows, d = c_pad.shape
    n = w_ada.shape[1]
    return pl.pallas_call(
        _mod_kernel,
        grid=(n // tn,),
        in_specs=[
            pl.BlockSpec((rows, d), lambda j: (0, 0)),
            pl.BlockSpec((d, tn), lambda j: (0, j)),
            pl.BlockSpec((1, tn), lambda j: (0, j)),
        ],
        out_specs=pl.BlockSpec((rows, tn), lambda j: (0, j)),
        out_shape=jax.ShapeDtypeStruct((rows, n), F32),
        compiler_params=_params("parallel"),
        name="mod",
    )(c_pad, w_ada, b_ada)


def _ffn_kernel(h_ref, mod_ref, g_ref, b_ref, win_hbm, wout_hbm, o_ref, u_ref, wa_buf, wb_buf, wo_buf, sem,
                *, rows, alpha, out_chunks, n_tiles, nj):
    sh, sc, gt = rows
    tf = wo_buf.shape[1]
    base = pl.program_id(0) * nj
    total = n_tiles * nj

    def copies(s):
        j = lax.rem(s, nj)
        slot = lax.rem(s, FFN_RING)
        col_a = pl.multiple_of(j * tf, tf)
        col_b = pl.multiple_of((nj + j) * tf, tf)
        return (
            pltpu.make_async_copy(win_hbm.at[:, pl.ds(col_a, tf)], wa_buf.at[slot], sem.at[0, slot]),
            pltpu.make_async_copy(win_hbm.at[:, pl.ds(col_b, tf)], wb_buf.at[slot], sem.at[1, slot]),
            pltpu.make_async_copy(wout_hbm.at[pl.ds(col_a, tf), :], wo_buf.at[slot], sem.at[2, slot]),
        )

    def start(s):
        for c in copies(s):
            c.start()

    @pl.when(pl.program_id(0) == 0)
    def _():
        for s in range(FFN_RING - 1):
            start(s)

    def begin_step(s):
        @pl.when(s + (FFN_RING - 1) < total)
        def _():
            start(s + (FFN_RING - 1))

        for c in copies(s):
            c.wait()
        return lax.rem(s, FFN_RING)

    def gated(u, slot):
        a = jnp.dot(u, wa_buf[slot], preferred_element_type=F32)
        b = jnp.dot(u, wb_buf[slot], preferred_element_type=F32)
        return ((a * b) / (1.0 + jnp.exp(-a))).astype(BF16)

    slot = begin_step(base)
    u = (h_ref[...] * (1.0 + mod_ref[0, sc:sc + 1, :]) + mod_ref[0, sh:sh + 1, :]).astype(BF16)
    u_ref[...] = u
    o_ref[...] = jnp.dot(gated(u, slot), wo_buf[slot], preferred_element_type=F32)

    def middle(j, carry):
        slot = begin_step(base + j)
        o_ref[...] += jnp.dot(gated(u_ref[...], slot), wo_buf[slot], preferred_element_type=F32)
        return carry

    lax.fori_loop(1, nj - 1, middle, 0)

    slot = begin_step(base + (nj - 1))
    g = gated(u_ref[...], slot)
    tr = o_ref.shape[0] // out_chunks
    for r in range(out_chunks):
        rs = slice(r * tr, (r + 1) * tr)
        f = o_ref[rs, :] + jnp.dot(g[rs], wo_buf[slot], preferred_element_type=F32)
        o_ref[rs, :] = _deepnorm(h_ref[rs, :], ((0.5 / alpha) * mod_ref[0, gt:gt + 1, :]) * f, alpha,
                                 g_ref[...], b_ref[...])


def _ffn_call(h, mod, w_in, w_out, ln_g, ln_b, rows, alpha, seq, tm=1024, tf=512):
    n, d = h.shape
    dff = w_out.shape[0]
    nj = dff // tf
    per_batch = seq // tm
    return pl.pallas_call(
        functools.partial(_ffn_kernel, rows=rows, alpha=alpha, out_chunks=FFN_OUT_CHUNKS, n_tiles=n // tm, nj=nj),
        grid=(n // tm,),
        in_specs=[
            pl.BlockSpec((tm, d), lambda i: (i, 0)),
            pl.BlockSpec((1, N_MOD, d), lambda i: (i // per_batch, 0, 0)),
            pl.BlockSpec((1, d), lambda i: (0, 0)),
            pl.BlockSpec((1, d), lambda i: (0, 0)),
            pl.BlockSpec(memory_space=pl.ANY),
            pl.BlockSpec(memory_space=pl.ANY),
        ],
        out_specs=pl.BlockSpec((tm, d), lambda i: (i, 0)),
        out_shape=jax.ShapeDtypeStruct((n, d), F32),
        scratch_shapes=[
            pltpu.VMEM((tm, d), BF16),
            pltpu.VMEM((FFN_RING, d, tf), BF16),
            pltpu.VMEM((FFN_RING, d, tf), BF16),
            pltpu.VMEM((FFN_RING, tf, d), BF16),
            pltpu.SemaphoreType.DMA((3, FFN_RING)),
        ],
        compiler_params=_params("arbitrary", vmem_limit_bytes=FFN_VMEM_LIMIT_BYTES),
        name="ffn",
    )(h, mod, ln_g, ln_b, w_in, w_out)


def _repack_kernel(w_ref, wlr_ref, main_ref, lr_ref):
    main_ref[...] = w_ref[...].astype(BF16)
    row = lax.broadcasted_iota(jnp.int32, lr_ref.shape, 0)
    lr_ref[...] = jnp.where(row < GATE_RANK, wlr_ref[...], 0.0).astype(BF16)


def _repack_call(w_t, n_main, tr=1024):
    rows, d = w_t.shape
    main_blocks = n_main // tr
    src = lambda k: (SUBLANES * (k * (tr // SUBLANES) + jnp.where(k < main_blocks, 0, GATE_RANK // SUBLANES)), 0)
    return pl.pallas_call(
        _repack_kernel,
        grid=((rows - GATE_RANK) // tr,),
        in_specs=[
            pl.BlockSpec((pl.Element(tr), pl.Element(d)), src),
            pl.BlockSpec((pl.Element(LANES), pl.Element(d)), lambda k: (n_main, 0)),
        ],
        out_specs=[
            pl.BlockSpec((tr, d), lambda k: (k, 0)),
            pl.BlockSpec((LANES, d), lambda k: (0, 0)),
        ],
        out_shape=[
            jax.ShapeDtypeStruct((rows - GATE_RANK, d), BF16),
            jax.ShapeDtypeStruct((LANES, d), BF16),
        ],
        compiler_params=_params("arbitrary"),
        name="repack",
    )(w_t, w_t)


_NT = (((1,), (1,)), ((), ()))


def _mix_in_kernel(h_ref, mod_ref, w_ref, wlr_ref, p_ref, lr_ref, u_ref):
    j = pl.program_id(1)

    @pl.when(j == 0)
    def _():
        u = (h_ref[...] * (1.0 + mod_ref[0, 4:5, :]) + mod_ref[0, 3:4, :]).astype(BF16)
        u_ref[...] = u
        p_ref[...] = lax.dot_general(u, w_ref[...], _NT, preferred_element_type=F32).astype(BF16)
        lr_ref[...] = lax.dot_general(u, wlr_ref[...], _NT, preferred_element_type=F32)

    @pl.when(j > 0)
    def _():
        p_ref[...] = lax.dot_general(u_ref[...], w_ref[...], _NT, preferred_element_type=F32).astype(BF16)


def _mix_in_call(h, mod, w_main_t, w_lr_t, seq, tm=1024, tn=2048):
    n, d = h.shape
    cols = w_main_t.shape[0]
    per_batch = seq // tm
    return pl.pallas_call(
        _mix_in_kernel,
        grid=(n // tm, cols // tn),
        in_specs=[
            pl.BlockSpec((tm, d), lambda i, j: (i, 0)),
            pl.BlockSpec((1, N_MOD, d), lambda i, j: (i // per_batch, 0, 0)),
            pl.BlockSpec((tn, d), lambda i, j: (j, 0)),
            pl.BlockSpec((LANES, d), lambda i, j: (0, 0)),
        ],
        out_specs=[
            pl.BlockSpec((tm, tn), lambda i, j: (i, j)),
            pl.BlockSpec((tm, LANES), lambda i, j: (i, 0)),
        ],
        out_shape=[
            jax.ShapeDtypeStruct((n, cols), BF16),
            jax.ShapeDtypeStruct((n, LANES), F32),
        ],
        scratch_shapes=[pltpu.VMEM((tm, d), BF16)],
        compiler_params=_params("parallel", "arbitrary"),
        name="mix_in",
    )(h, mod, w_main_t, w_lr_t)


def _bias_table_kernel(e_ref, t_ref):
    row = lax.broadcasted_iota(jnp.int32, (Q_SUB, K_WIN), 0)
    col = lax.broadcasted_iota(jnp.int32, (Q_SUB, K_WIN), 1)
    q_chunk = row // CHUNK
    k_chunk = col // CHUNK
    in_band = (k_chunk >= q_chunk) & (k_chunk <= q_chunk + A_PAST_CHUNKS)
    for h in range(2):
        e = jnp.broadcast_to(e_ref[h], (Q_SUB, E_LEN))
        t = pltpu.roll(e, E_LEN - (Q_SUB - 1), 1, stride=1, stride_axis=0)[:, :K_WIN]
        t_ref[0, h * Q_SUB:(h + 1) * Q_SUB, :] = jnp.where(in_band, t, -jnp.inf)


def _bias_table_call(e_ext):
    pairs = e_ext.shape[0] // 2
    return pl.pallas_call(
        _bias_table_kernel,
        grid=(pairs,),
        in_specs=[pl.BlockSpec((2, 1, E_LEN), lambda hp: (hp, 0, 0))],
        out_specs=pl.BlockSpec((1, 2 * Q_SUB, K_WIN), lambda hp: (hp, 0, 0)),
        out_shape=jax.ShapeDtypeStruct((pairs, 2 * Q_SUB, K_WIN), F32),
        compiler_params=_params("parallel"),
        name="bias_table",
    )(e_ext)


def _attn_kernel(q_ref, k_ref, v_ref, t_ref, *refs, seq):
    n_cast = len(refs) // 2
    o_ref = refs[n_cast]
    for src_ref, dst_ref in zip(refs[:n_cast], refs[n_cast + 1:]):
        dst_ref[...] = src_ref[...].astype(dst_ref.dtype)

    lane = lax.broadcasted_iota(jnp.int32, (Q_SUB, 2 * A_HEAD_DIM), 1)
    head0 = lane < A_HEAD_DIM

    def sub_tile(q_start, k_start, width, t_start):
        q = q_ref[pl.ds(q_start, Q_SUB), :].astype(F32) * (A_HEAD_DIM ** -0.5)
        q2 = jnp.concatenate([jnp.where(head0, q, 0.0), jnp.where(head0, 0.0, q)], axis=0).astype(BF16)
        kw = k_ref[pl.ds(k_start, width), :]
        vw = v_ref[pl.ds(k_start, width), :]
        s = lax.dot_general(q2, kw, _NT, preferred_element_type=F32)
        s = s + t_ref[0, :, t_start:t_start + width]
        m = jnp.max(s, axis=-1, keepdims=True)
        p = jnp.exp(s - m)
        l = jnp.sum(p, axis=-1, keepdims=True)
        o = jnp.dot(p.astype(BF16), vw, preferred_element_type=F32) / l
        o_ref[pl.ds(q_start, Q_SUB), :] = jnp.where(head0, o[:Q_SUB], o[Q_SUB:]).astype(o_ref.dtype)

    n_head = (K_WIN - Q_SUB) // Q_SUB
    for j in range(n_head):
        sub_tile(j * Q_SUB, 0, (j + 1) * Q_SUB, K_WIN - (j + 1) * Q_SUB)

    def body(j, carry):
        q_start = pl.multiple_of(j * Q_SUB, Q_SUB)
        k_start = pl.multiple_of(j * Q_SUB - (K_WIN - Q_SUB), Q_SUB)
        sub_tile(q_start, k_start, K_WIN, 0)
        return carry

    lax.fori_loop(n_head, seq // Q_SUB, body, 0, unroll=ATTN_UNROLL)


def _attn_call(p, table, batch, seq, cast_weights):
    n = p.shape[0]
    pairs = A_HEADS // 2
    w = 2 * A_HEAD_DIM
    steps = batch * pairs
    slab = lambda a: pl.BlockSpec((a.shape[0] // steps, a.shape[1]), lambda b, hp: (b * pairs + hp, 0))
    outs = pl.pallas_call(
        functools.partial(_attn_kernel, seq=seq),
        grid=(batch, pairs),
        in_specs=[
            pl.BlockSpec((seq, w), lambda b, hp: (b, hp)),
            pl.BlockSpec((seq, w), lambda b, hp: (b, pairs + hp)),
            pl.BlockSpec((seq, w), lambda b, hp: (b, 2 * pairs + hp)),
            pl.BlockSpec((1, 2 * Q_SUB, K_WIN), lambda b, hp: (hp, 0, 0)),
        ] + [slab(a) for a in cast_weights],
        out_specs=[pl.BlockSpec((seq, w), lambda b, hp: (b, hp))] + [slab(a) for a in cast_weights],
        out_shape=[jax.ShapeDtypeStruct((n, A_WIDTH), BF16)]
        + [jax.ShapeDtypeStruct(a.shape, BF16) for a in cast_weights],
        compiler_params=_params("parallel", "parallel"),
        name="attn",
    )(p, p, p, table, *cast_weights)
    return outs[0], outs[1:]


def _gla_kernel(q_ref, k_ref, v_ref, r_ref, lr_ref, wa_ref, ba_ref, gn_ref, later_ref, member_ref, o_ref, st_ref,
                readout_ref,
                *, chunks, hk, hv):
    @pl.when(pl.program_id(1) == 0)
    def _():
        st_ref[...] = jnp.zeros_like(st_ref)

    rows = chunks * CHUNK
    z = jnp.dot(lr_ref[...].astype(BF16), wa_ref[...], preferred_element_type=F32) + ba_ref[...]
    log_a = -(jnp.maximum(-z, 0.0) + jnp.log(1.0 + jnp.exp(-jnp.abs(z)))) / GATE_TAU
    x0 = log_a.astype(BF16)
    x1 = (log_a - x0.astype(F32)).astype(BF16)
    later = later_ref[...]
    member = member_ref[...]
    rest = jnp.dot(later, x0, preferred_element_type=F32) + jnp.dot(later, x1, preferred_element_type=F32)
    total = jnp.dot(member, x0, preferred_element_type=F32) + jnp.dot(member, x1, preferred_element_type=F32)
    decay = jnp.exp(total)
    kdec = (k_ref[...].astype(F32) * jnp.exp(rest)).astype(BF16)

    for n in range(chunks):
        rs = slice(n * CHUNK, (n + 1) * CHUNK)
        for h in range(B_HEADS):
            ks = slice(h * hk, (h + 1) * hk)
            vs = slice(h * hv, (h + 1) * hv)
            upd = lax.dot_general(v_ref[rs, vs], kdec[rs, ks], (((0,), (0,)), ((), ())),
                                  preferred_element_type=F32)
            st = st_ref[h] * decay[n:n + 1, ks] + upd
            st_ref[h] = st
            readout_ref[rs, vs] = lax.dot_general(q_ref[rs, ks], st.astype(BF16), _NT, preferred_element_type=F32)

    for h in range(B_HEADS):
        vs = slice(h * hv, (h + 1) * hv)
        o = readout_ref[:, vs]
        o = o * lax.rsqrt(jnp.mean(o * o, axis=-1, keepdims=True) + RMS_EPS * hk) * gn_ref[...]
        o_ref[:, vs] = (o * _silu(r_ref[:, vs].astype(F32))).astype(o_ref.dtype)


def _gla_call(p, lr, w_alpha2, b_alpha, gn, batch, seq, kdim, vdim, rows=512):
    n = p.shape[0]
    steps = seq // rows
    hk, hv = kdim // B_HEADS, vdim // B_HEADS
    q_blk = (3 * A_WIDTH) // kdim
    v_blk = (3 * A_WIDTH + 2 * kdim) // vdim
    row_map = lambda col: (lambda b, t: (b * steps + t, col))
    frame = jnp.arange(rows)
    same_chunk = frame[:, None] // CHUNK == frame[None, :] // CHUNK
    later = jnp.logical_and(same_chunk, frame[None, :] > frame[:, None]).astype(BF16)
    member = (frame[None, :] // CHUNK == jnp.arange(rows // CHUNK)[:, None]).astype(BF16)
    return pl.pallas_call(
        functools.partial(_gla_kernel, chunks=rows // CHUNK, hk=hk, hv=hv),
        grid=(batch, steps),
        in_specs=[
            pl.BlockSpec((rows, kdim), row_map(q_blk)),
            pl.BlockSpec((rows, kdim), row_map(q_blk + 1)),
            pl.BlockSpec((rows, vdim), row_map(v_blk)),
            pl.BlockSpec((rows, vdim), row_map(v_blk + 1)),
            pl.BlockSpec((rows, LANES), row_map(0)),
            pl.BlockSpec((LANES, kdim), lambda b, t: (0, 0)),
            pl.BlockSpec((1, kdim), lambda b, t: (0, 0)),
            pl.BlockSpec((1, hv), lambda b, t: (0, 0)),
            pl.BlockSpec((rows, rows), lambda b, t: (0, 0)),
            pl.BlockSpec((rows // CHUNK, rows), lambda b, t: (0, 0)),
        ],
        out_specs=pl.BlockSpec((rows, vdim), row_map(0)),
        out_shape=jax.ShapeDtypeStruct((n, vdim), BF16),
        scratch_shapes=[pltpu.VMEM((B_HEADS, hv, hk), F32), pltpu.VMEM((rows, vdim), F32)],
        compiler_params=_params("parallel", "arbitrary"),
        name="gla",
    )(p, p, p, p, lr, w_alpha2, b_alpha, gn, later, member)


def _merge_kernel(ya_ref, yb_ref, ga_ref, gb_ref, h_ref, mod_ref, wpa_ref, wpb_ref, wo_ref, g_ref, b_ref, o_ref,
                  *, alpha, chunks):
    tr = o_ref.shape[0] // chunks
    for r in range(chunks):
        rs = slice(r * tr, (r + 1) * tr)
        pa = jnp.dot(ya_ref[rs, :], wpa_ref[...], preferred_element_type=F32)
        pb = jnp.dot(yb_ref[rs, :], wpb_ref[...], preferred_element_type=F32)
        merged = (jax.nn.sigmoid(ga_ref[rs, :].astype(F32)) * pa
                  + jax.nn.sigmoid(gb_ref[rs, :].astype(F32)) * pb)
        m = jnp.dot(merged.astype(BF16), wo_ref[...], preferred_element_type=F32)
        o_ref[rs, :] = _deepnorm(h_ref[rs, :], (mod_ref[0, 5:6, :] / alpha) * m, alpha, g_ref[...], b_ref[...])


def _merge_call(ya, yb, p, h, mod, w_pa, w_pb, w_out, ln_g, ln_b, alpha, seq, tm=512, chunks=2):
    n, d = h.shape
    per_batch = seq // tm
    ga_blk = (p.shape[1] - 2 * d) // d
    const = lambda shape: pl.BlockSpec(shape, lambda i: (0, 0), pipeline_mode=pl.Buffered(1))
    return pl.pallas_call(
        functools.partial(_merge_kernel, alpha=alpha, chunks=chunks),
        grid=(n // tm,),
        in_specs=[
            pl.BlockSpec((tm, ya.shape[1]), lambda i: (i, 0)),
            pl.BlockSpec((tm, yb.shape[1]), lambda i: (i, 0)),
            pl.BlockSpec((tm, d), lambda i: (i, ga_blk)),
            pl.BlockSpec((tm, d), lambda i: (i, ga_blk + 1)),
            pl.BlockSpec((tm, d), lambda i: (i, 0)),
            pl.BlockSpec((1, N_MOD, d), lambda i: (i // per_batch, 0, 0)),
            const(w_pa.shape),
            const(w_pb.shape),
            const(w_out.shape),
            const((1, d)),
            const((1, d)),
        ],
        out_specs=pl.BlockSpec((tm, d), lambda i: (i, 0)),
        out_shape=jax.ShapeDtypeStruct((n, d), F32),
        compiler_params=_params("parallel"),
        name="merge",
    )(ya, yb, p, p, h, mod, w_pa, w_pb, w_out, ln_g, ln_b)


def _extended_rel_bias(rel_bias):
    heads, rel_size = rel_bias.shape
    n_low = (Q_SUB - 1) + (K_WIN - Q_SUB) - REL_CLIP
    n_high = E_LEN - n_low - rel_size
    e = jnp.concatenate([
        jnp.broadcast_to(rel_bias[:, :1], (heads, n_low)),
        rel_bias,
        jnp.broadcast_to(rel_bias[:, -1:], (heads, n_high)),
    ], axis=1)
    return e.reshape(heads, 1, E_LEN).astype(F32)


def kernel(x, c, w_ada, b_ada, ffn1_w_in, ffn1_w_out, ln1_g, ln1_b, w_mix_in, rel_bias, w_alpha2, b_alpha,
           gla_norm_g, w_proj_a, w_proj_b, w_mix_out, ln2_g, ln2_b, ffn2_w_in, ffn2_w_out, ln3_g, ln3_b):
    batch, seq, d = x.shape
    depth = w_ada.shape[0]
    alpha = (2.0 * depth) ** 0.25
    kdim = w_alpha2.shape[2]
    vdim = w_proj_b.shape[1]
    n_main = 3 * A_WIDTH + 2 * kdim + 2 * vdim
    row = lambda v: v.reshape(1, -1)

    h = x.reshape(batch * seq, d)
    c_pad = jnp.zeros((BF16_SUBLANES, d), F32).at[:batch].set(c)
    for l in range(depth):
        mod = _mod_call(c_pad, w_ada[l], row(b_ada[l]))[:batch].reshape(batch, N_MOD, d)

        h = _ffn_call(h, mod, ffn1_w_in[l].astype(BF16), ffn1_w_out[l].astype(BF16), row(ln1_g[l]), row(ln1_b[l]),
                      (0, 1, 2), alpha, seq)

        w_main_t, w_lr_t = _repack_call(jnp.swapaxes(w_mix_in, 1, 2)[l], n_main)
        p, lr = _mix_in_call(h, mod, w_main_t, w_lr_t, seq)

        ya, (w_pa, w_pb, w_mo, w2_in, w2_out) = _attn_call(
            p, _bias_table_call(_extended_rel_bias(rel_bias[l])), batch, seq,
            (w_proj_a[l], w_proj_b[l], w_mix_out[l], ffn2_w_in[l], ffn2_w_out[l]))
        w_a2 = jnp.zeros((LANES, kdim), BF16).at[:GATE_RANK].set(w_alpha2[l].astype(BF16))
        yb = _gla_call(p, lr, w_a2, row(b_alpha[l]), row(gla_norm_g[l]), batch, seq, kdim, vdim)

        h = _merge_call(ya, yb, p, h, mod, w_pa, w_pb, w_mo, row(ln2_g[l]), row(ln2_b[l]), alpha, seq)

        h = _ffn_call(h, mod, w2_in, w2_out, row(ln3_g[l]), row(ln3_b[l]), (6, 7, 8), alpha, seq)
    return h.reshape(batch, seq, d)
```
